```python
import jax, jax.numpy as jnp
from jax import lax
import numpy as np

D_MODEL = 1024
BATCH = 16
SEQ = 2048
DEPTH = 1

HEAD_DIM = 64
N_HEADS = D_MODEL // HEAD_DIM
N_MOBA = N_HEADS // 2
N_FOX = N_HEADS - N_MOBA
W_MOBA = N_MOBA * HEAD_DIM
W_FOX = N_FOX * HEAD_DIM
MOBA_BLOCK = 256
MOBA_TOPK = 3
MOBA_QCHUNK = 16
FOX_QBLOCK = 128
N_BUCKETS = 32
MAX_DISTANCE = 128
D_FF = 4 * D_MODEL
PLE_DIM = 256
EPS = 1e-6
NEG = -1e30
SCALE = HEAD_DIM ** -0.5
IN_WIDTH = 3 * W_MOBA + 3 * W_FOX + N_FOX

kernel_name = "hymba_moba_fox_ple_layer"


def rmsnorm(x, g):
    xf = x.astype(jnp.float32)
    y = xf * lax.rsqrt(jnp.mean(xf * xf, axis=-1, keepdims=True) + EPS)
    return (y * g.astype(jnp.float32)).astype(x.dtype)


def t5_bucket(rel):
    rel = jnp.maximum(rel, 0)
    max_exact = N_BUCKETS // 2
    relf = jnp.maximum(rel, max_exact).astype(jnp.float32)
    large = max_exact + (jnp.log(relf / max_exact) / np.log(MAX_DISTANCE / max_exact)
                         * (N_BUCKETS - max_exact)).astype(jnp.int32)
    large = jnp.minimum(large, N_BUCKETS - 1)
    return jnp.where(rel < max_exact, rel, large)


def moba_attention(q, k, v, rel_bias):
    B, H, S, Dh = q.shape
    nb = -(-S // MOBA_BLOCK)
    pad = nb * MOBA_BLOCK - S
    k_sel = min(MOBA_TOPK, nb)
    kb = jnp.pad(k, ((0, 0), (0, 0), (0, pad), (0, 0))).reshape(B, H, nb, MOBA_BLOCK, Dh)
    vb = jnp.pad(v, ((0, 0), (0, 0), (0, pad), (0, 0))).reshape(B, H, nb, MOBA_BLOCK, Dh)
    kmean = jnp.mean(kb.astype(jnp.float32), axis=3)
    relT = rel_bias.T
    bi = jnp.arange(B)[:, None, None, None]
    hi = jnp.arange(H)[None, :, None, None]
    offs = jnp.arange(MOBA_BLOCK)
    blk_ids = jnp.arange(nb)
    n_sel = k_sel * MOBA_BLOCK

    def chunk(c):
        t0 = c * MOBA_QCHUNK
        qc = lax.dynamic_slice_in_dim(q, t0, MOBA_QCHUNK, axis=2)
        tq = t0 + jnp.arange(MOBA_QCHUNK)
        own = t0 // MOBA_BLOCK
        gate = jnp.einsum('bhqd,bhnd->bhqn', qc.astype(jnp.float32), kmean)
        gate = jnp.where(blk_ids < own, gate, NEG)
        _, idx = lax.top_k(gate, k_sel)
        valid = idx < own
        ksel = kb[bi, hi, idx]
        vsel = vb[bi, hi, idx]
        s_sel = jnp.einsum('bhqd,bhqkld->bhqkl', qc, ksel).astype(jnp.float32) * SCALE
        pos_sel = idx[..., None] * MOBA_BLOCK + offs
        bucket_sel = t5_bucket(tq[None, None, :, None, None] - pos_sel)
        s_sel = s_sel + relT[hi[..., None], bucket_sel].astype(jnp.float32)
        s_sel = jnp.where(valid[..., None], s_sel, NEG).reshape(B, H, MOBA_QCHUNK, n_sel)
        kown = lax.dynamic_index_in_dim(kb, own, axis=2, keepdims=False)
        vown = lax.dynamic_index_in_dim(vb, own, axis=2, keepdims=False)
        rel_own = tq[:, None] - (own * MOBA_BLOCK + offs)[None, :]
        s_own = jnp.einsum('bhqd,bhld->bhql', qc, kown).astype(jnp.float32) * SCALE
        s_own = s_own + relT[:, t5_bucket(rel_own)].astype(jnp.float32)[None]
        s_own = jnp.where(rel_own >= 0, s_own, NEG)
        probs = jax.nn.softmax(jnp.concatenate([s_sel, s_own], axis=-1), axis=-1)
        p_sel = probs[..., :n_sel].reshape(B, H, MOBA_QCHUNK, k_sel, MOBA_BLOCK)
        p_own = probs[..., n_sel:]
        out = (jnp.einsum('bhqkl,bhqkld->bhqd', p_sel.astype(v.dtype), vsel)
               + jnp.einsum('bhql,bhld->bhqd', p_own.astype(v.dtype), vown))
        return out

    out = lax.map(chunk, jnp.arange(S // MOBA_QCHUNK))
    return jnp.moveaxis(out, 0, 2).reshape(B, H, S, Dh)


def forgetting_attention(q, k, v, log_f):
    B, H, S, Dh = q.shape
    cum = jnp.cumsum(log_f, axis=-1)
    pos = jnp.arange(S)

    def block(i):
        t0 = i * FOX_QBLOCK
        qb = lax.dynamic_slice_in_dim(q, t0, FOX_QBLOCK, axis=2)
        cq = lax.dynamic_slice_in_dim(cum, t0, FOX_QBLOCK, axis=2)
        tq = t0 + jnp.arange(FOX_QBLOCK)
        s = jnp.einsum('bhqd,bhkd->bhqk', qb, k).astype(jnp.float32) * SCALE
        s = s + cq[..., None] - cum[:, :, None, :]
        s = jnp.where(tq[:, None] >= pos[None, :], s, NEG)
        probs = jax.nn.softmax(s, axis=-1)
        return jnp.einsum('bhqk,bhkd->bhqd', probs.astype(v.dtype), v)

    out = lax.map(block, jnp.arange(S // FOX_QBLOCK))
    return jnp.moveaxis(out, 0, 2).reshape(B, H, S, Dh)


def setup_inputs(seed: int = 0) -> dict:
    key = jax.random.key(seed)
    ks = jax.random.split(key, 20)
    f32 = jnp.float32
    nrm = lambda k, shape, s: jax.random.normal(k, shape, f32) * s
    gain = lambda k, shape: 1.0 + 0.05 * jax.random.normal(k, shape, f32)
    return {
        "x": nrm(ks[0], (BATCH, SEQ, D_MODEL), 1.0),
        "p": nrm(ks[1], (DEPTH, BATCH, SEQ, PLE_DIM), 1.0),
        "rel_bias": nrm(ks[2], (N_BUCKETS, N_MOBA), 0.5),
        "g_attn": gain(ks[3], (DEPTH, D_MODEL)),
        "w_in": nrm(ks[4], (DEPTH, D_MODEL, IN_WIDTH), D_MODEL ** -0.5),
        "b_f": 3.0 + 0.1 * jax.random.normal(ks[5], (DEPTH, N_FOX), f32),
        "gq_moba": gain(ks[6], (DEPTH, HEAD_DIM)),
        "gk_moba": gain(ks[7], (DEPTH, HEAD_DIM)),
        "gq_fox": gain(ks[8], (DEPTH, HEAD_DIM)),
        "gk_fox": gain(ks[9], (DEPTH, HEAD_DIM)),
        "w_out": nrm(ks[10], (DEPTH, D_MODEL, D_MODEL), D_MODEL ** -0.5),
        "g_mlp": gain(ks[11], (DEPTH, D_MODEL)),
        "w_up": nrm(ks[12], (DEPTH, D_MODEL, D_FF), D_MODEL ** -0.5),
        "w_down": nrm(ks[13], (DEPTH, D_FF, D_MODEL), D_FF ** -0.5),
        "g_ple": gain(ks[14], (DEPTH, D_MODEL)),
        "w_ple_gate": nrm(ks[15], (DEPTH, D_MODEL, D_MODEL), D_MODEL ** -0.5),
        "w_ple_proj": nrm(ks[16], (DEPTH, PLE_DIM, D_MODEL), PLE_DIM ** -0.5),
    }


def reference(x, p, rel_bias, g_attn, w_in, b_f, gq_moba, gk_moba, gq_fox, gk_fox,
              w_out, g_mlp, w_up, w_down, g_ple, w_ple_gate, w_ple_proj):
    B, S, D = x.shape

    def heads(t, n):
        return t.reshape(B, S, n, HEAD_DIM).transpose(0, 2, 1, 3)

    for i in range(DEPTH):
        h = rmsnorm(x, g_attn[i])
        proj = h @ w_in[i]
        c0 = 0
        qm = proj[..., c0:c0 + W_MOBA]; c0 += W_MOBA
        km = proj[..., c0:c0 + W_MOBA]; c0 += W_MOBA
        vm = proj[..., c0:c0 + W_MOBA]; c0 += W_MOBA
        qf = proj[..., c0:c0 + W_FOX]; c0 += W_FOX
        kf = proj[..., c0:c0 + W_FOX]; c0 += W_FOX
        vf = proj[..., c0:c0 + W_FOX]; c0 += W_FOX
        f_logit = proj[..., c0:c0 + N_FOX]

        qm = rmsnorm(heads(qm, N_MOBA), gq_moba[i])
        km = rmsnorm(heads(km, N_MOBA), gk_moba[i])
        vm = heads(vm, N_MOBA)
        qf = rmsnorm(heads(qf, N_FOX), gq_fox[i])
        kf = rmsnorm(heads(kf, N_FOX), gk_fox[i])
        vf = heads(vf, N_FOX)
        log_f = jax.nn.log_sigmoid((f_logit + b_f[i]).astype(jnp.float32)).transpose(0, 2, 1)

        o_moba = moba_attention(qm, km, vm, rel_bias)
        o_fox = forgetting_attention(qf, kf, vf, log_f)
        o = jnp.concatenate([o_moba, o_fox], axis=1).transpose(0, 2, 1, 3).reshape(B, S, D)
        x = x + o @ w_out[i]

        h = rmsnorm(x, g_mlp[i])
        x = x + jnp.square(jax.nn.relu(h @ w_up[i])) @ w_down[i]

        gate = jax.nn.sigmoid(rmsnorm(x, g_ple[i]) @ w_ple_gate[i])
        x = x + gate * (p[i] @ w_ple_proj[i])
    return x
```

```python
import functools

import numpy as np
import jax
import jax.numpy as jnp
from jax import lax
from jax.experimental import pallas as pl
from jax.experimental.pallas import tpu as pltpu

HEAD_DIM = 64
MOBA_BLOCK = 256
MOBA_TOPK = 3
N_BUCKETS = 32
MAX_DISTANCE = 128
EPS = 1e-6
NEG = -1e30
SCALE = HEAD_DIM ** -0.5

LANES = 128
HEADS_PER_TILE = LANES // HEAD_DIM
PROJ_CHUNK = 512
VMEM_LIMIT = 56 * 1024 * 1024

_F32 = jnp.float32
_BF16 = jnp.bfloat16


def _nt(a, b):
    return lax.dot_general(a, b, (((1,), (1,)), ((), ())), preferred_element_type=_F32)


def _dot(a, b):
    return jnp.dot(a, b, preferred_element_type=_F32)


def _t5_bucket(rel):
    rel = jnp.maximum(rel, 0)
    max_exact = N_BUCKETS // 2
    relf = jnp.maximum(rel, max_exact).astype(_F32)
    large = max_exact + (jnp.log(relf / max_exact) / np.log(MAX_DISTANCE / max_exact)
                         * (N_BUCKETS - max_exact)).astype(jnp.int32)
    large = jnp.minimum(large, N_BUCKETS - 1)
    return jnp.where(rel < max_exact, rel, large)


def _bias_kernel(rb_ref, bown_ref, bprev_ref, out_ref):
    h = pl.program_id(0)
    bown = bown_ref[...]
    bprev = bprev_ref[...]
    t_own = jnp.zeros(bown.shape, _F32)
    t_prev = jnp.zeros(bprev.shape, _F32)
    for b in range(N_BUCKETS):
        v = rb_ref[b, h]
        t_own = jnp.where(bown == b, v, t_own)
        t_prev = jnp.where(bprev == b, v, t_prev)
    out_ref[0, 0] = jnp.where(bown < 0, NEG, t_own)
    out_ref[0, 1] = t_prev


def _bias_tables(rel_bias):
    n_heads = rel_bias.shape[1]
    t = jnp.arange(MOBA_BLOCK, dtype=jnp.int32)
    rel = t[:, None] - t[None, :]
    bown = jnp.where(rel >= 0, _t5_bucket(rel), -1).astype(jnp.int32)
    bprev = _t5_bucket(rel + MOBA_BLOCK).astype(jnp.int32)
    blk = (MOBA_BLOCK, MOBA_BLOCK)
    return pl.pallas_call(
        _bias_kernel,
        grid=(n_heads,),
        in_specs=[pl.BlockSpec(memory_space=pltpu.SMEM),
                  pl.BlockSpec(blk, lambda h: (0, 0)),
                  pl.BlockSpec(blk, lambda h: (0, 0))],
        out_specs=pl.BlockSpec((1, 2) + blk, lambda h: (h, 0, 0, 0)),
        out_shape=jax.ShapeDtypeStruct((n_heads, 2) + blk, _F32),
        name="moba_bias_tables",
    )(rel_bias, bown, bprev)


def _proj_kernel(x_ref, g_ref, w_ref, wf_ref, bf_ref, gain_ref, tri_ref,
                 qkv_ref, cumc_ref, cumr_ref, carry_ref, *, norm_chunks):
    tm = x_ref.shape[1]
    n_f = cumc_ref.shape[2]

    @pl.when(pl.program_id(1) == 0)
    def _():
        carry_ref[...] = jnp.zeros_like(carry_ref)

    x = x_ref[0]
    ms = jnp.mean(x * x, axis=-1, keepdims=True)
    h = (x * lax.rsqrt(ms + EPS) * g_ref[...]).astype(_BF16)

    lane = lax.broadcasted_iota(jnp.int32, (1, LANES), 1)
    lo_mask = lane < HEAD_DIM
    for c, do_norm in enumerate(norm_chunks):
        acc = _dot(h, w_ref[:, c * PROJ_CHUNK:(c + 1) * PROJ_CHUNK])
        for s in range(PROJ_CHUNK // LANES):
            c0 = c * PROJ_CHUNK + s * LANES
            y = acc[:, s * LANES:(s + 1) * LANES]
            if do_norm:
                sq = y * y
                s_lo = jnp.sum(jnp.where(lo_mask, sq, 0.0), axis=-1, keepdims=True)
                s_hi = jnp.sum(jnp.where(lo_mask, 0.0, sq), axis=-1, keepdims=True)
                msq = jnp.where(lo_mask, s_lo, s_hi) * (1.0 / HEAD_DIM)
                y = y * lax.rsqrt(msq + EPS) * gain_ref[:, c0:c0 + LANES]
            qkv_ref[0, :, c0:c0 + LANES] = y.astype(_BF16)

    z = _dot(h, wf_ref[...]) + bf_ref[...]
    lf = jnp.minimum(z, 0.0) - jnp.log1p(jnp.exp(-jnp.abs(z)))
    hi = lf.astype(_BF16)
    r1 = lf - hi.astype(_F32)
    mid = r1.astype(_BF16)
    lo = (r1 - mid.astype(_F32)).astype(_BF16)
    tri = tri_ref[...]
    cum = (_dot(tri, hi) + _dot(tri, mid)) + _dot(tri, lo) + carry_ref[...]
    carry_ref[...] = cum[tm - 1:tm, :]
    cumc_ref[0] = cum[:, :n_f]
    cumr_ref[0] = cum.T[:n_f, :]


def _proj_call(x, g_attn, w_main, w_f, b_f_row, gain_row, norm_chunks, n_fox, tm):
    B, S, D = x.shape
    W = w_main.shape[1]
    tri = jnp.tril(jnp.ones((tm, tm), _F32)).astype(_BF16)
    const = lambda shape: pl.BlockSpec(shape, lambda b, s: (0,) * len(shape))
    return pl.pallas_call(
        functools.partial(_proj_kernel, norm_chunks=norm_chunks),
        grid=(B, S // tm),
        in_specs=[pl.BlockSpec((1, tm, D), lambda b, s: (b, s, 0)),
                  const((1, D)), const((D, W)), const((D, LANES)), const((1, LANES)),
                  const((1, W)), const((tm, tm))],
        out_specs=[pl.BlockSpec((1, tm, W), lambda b, s: (b, s, 0)),
                   pl.BlockSpec((1, tm, n_fox), lambda b, s: (b, s, 0)),
                   pl.BlockSpec((1, n_fox, tm), lambda b, s: (b, 0, s))],
        out_shape=[jax.ShapeDtypeStruct((B, S, W), _BF16),
                   jax.ShapeDtypeStruct((B, S, n_fox), _F32),
                   jax.ShapeDtypeStruct((B, n_fox, S), _F32)],
        scratch_shapes=[pltpu.VMEM((1, LANES), _F32)],
        compiler_params=pltpu.CompilerParams(
            dimension_semantics=("parallel", "arbitrary"), vmem_limit_bytes=VMEM_LIMIT),
        name="proj_qknorm_cumsum",
    )(x, g_attn, w_main, w_f, b_f_row, gain_row, tri)


def _softmax_first(s, v):
    m = jnp.max(s, axis=-1, keepdims=True)
    p = jnp.exp(s - m)
    l = jnp.sum(p, axis=-1, keepdims=True)
    acc = _dot(p.astype(_BF16), v)
    return m, l, acc


def _softmax_next(s, v, m, l, acc):
    m_new = jnp.maximum(m, jnp.max(s, axis=-1, keepdims=True))
    alpha = jnp.exp(m - m_new)
    p = jnp.exp(s - m_new)
    l = alpha * l + jnp.sum(p, axis=-1, keepdims=True)
    acc = alpha * acc + _dot(p.astype(_BF16), v)
    return m_new, l, acc


def _store_head(o_ref, r0, tq, hl, lane, val):
    @pl.when(hl == 0)
    def _():
        o_ref[0, r0:r0 + tq, :] = val

    @pl.when(hl != 0)
    def _():
        prev = o_ref[0, r0:r0 + tq, :]
        o_ref[0, r0:r0 + tq, :] = jnp.where(lane < HEAD_DIM, prev, val)


def _moba_kernel(far_ref, q_ref, k_ref, v_ref, bias_ref, o_ref):
    S = q_ref.shape[1]
    tq = MOBA_BLOCK
    nb = S // tq
    hp = pl.program_id(1)
    lane = lax.broadcasted_iota(jnp.int32, (1, LANES), 1)

    km_rows = [jnp.mean(k_ref[0, j * tq:(j + 1) * tq, :].astype(_F32), axis=0, keepdims=True)
               for j in range(nb)]
    km_rows += [jnp.zeros((1, LANES), _F32)] * (-nb % 16)
    km = jnp.concatenate(km_rows, axis=0)
    nrow = km.shape[0]
    row = lax.broadcasted_iota(jnp.int32, (nrow, tq), 0)

    def head_body(hl, carry):
        hmask = (lane >= hl * HEAD_DIM) & (lane < (hl + 1) * HEAD_DIM)
        mvec = jnp.where(hmask, 1.0, 0.0)
        mvec16 = mvec.astype(_BF16)
        far = far_ref[hp * HEADS_PER_TILE + hl]
        kmh = km * mvec
        km_hi = kmh.astype(_BF16)
        km_lo = (kmh - km_hi.astype(_F32)).astype(_BF16)

        for qi in range(nb):
            r0 = qi * tq
            qh = q_ref[0, r0:r0 + tq, :] * mvec16
            s = _nt(qh, k_ref[0, r0:r0 + tq, :]) + bias_ref[hl, 0]
            m, l, acc = _softmax_first(s, v_ref[0, r0:r0 + tq, :])

            mask_col = None
            if qi > MOBA_TOPK:
                gt = _nt(km_hi, qh) + _nt(km_lo, qh)
                ge = jnp.where(row < qi, gt, -jnp.inf)
                mrows = []
                for j in range(qi):
                    gj = ge[j:j + 1, :]
                    ahead = (ge > gj) | ((ge == gj) & (row < j))
                    rank = jnp.sum(jnp.where(ahead, 1.0, 0.0), axis=0, keepdims=True)
                    mrows.append(jnp.where(rank < MOBA_TOPK, 0.0, NEG))
                mrows += [jnp.zeros((1, tq), _F32)] * (LANES - qi)
                mask_col = jnp.concatenate(mrows, axis=0).T

            for j in range(qi):
                s = _nt(qh, k_ref[0, j * tq:(j + 1) * tq, :])
                if j == qi - 1:
                    s = s + bias_ref[hl, 1]
                    if mask_col is not None:
                        s = s + mask_col[:, j:j + 1]
                elif mask_col is not None:
                    s = s + (mask_col[:, j:j + 1] + far)
                else:
                    s = s + far
                m, l, acc = _softmax_next(s, v_ref[0, j * tq:(j + 1) * tq, :], m, l, acc)

            _store_head(o_ref, r0, tq, hl, lane, (acc * (1.0 / l)).astype(o_ref.dtype))
        return carry

    lax.fori_loop(0, HEADS_PER_TILE, head_body, 0)


def _moba_call(qkv, bias_tab, rel_far, n_tiles, q_off, k_off, v_off):
    B, S, _ = qkv.shape
    blk = (1, S, LANES)
    return pl.pallas_call(
        _moba_kernel,
        grid=(B, n_tiles),
        in_specs=[pl.BlockSpec(memory_space=pltpu.SMEM),
                  pl.BlockSpec(blk, lambda b, t: (b, 0, q_off + t)),
                  pl.BlockSpec(blk, lambda b, t: (b, 0, k_off + t)),
                  pl.BlockSpec(blk, lambda b, t: (b, 0, v_off + t)),
                  pl.BlockSpec((HEADS_PER_TILE, 2, MOBA_BLOCK, MOBA_BLOCK),
                               lambda b, t: (t, 0, 0, 0))],
        out_specs=pl.BlockSpec(blk, lambda b, t: (b, 0, t)),
        out_shape=jax.ShapeDtypeStruct((B, S, n_tiles * LANES), _BF16),
        compiler_params=pltpu.CompilerParams(
            dimension_semantics=("parallel", "arbitrary"), vmem_limit_bytes=VMEM_LIMIT),
        name="moba_attention",
    )(rel_far, qkv, qkv, qkv, bias_tab)


def _fox_kernel(q_ref, k_ref, v_ref, cc_ref, cr_ref, causal_ref, o_ref):
    S = q_ref.shape[1]
    tq = causal_ref.shape[0]
    nb = S // tq
    lane = lax.broadcasted_iota(jnp.int32, (1, LANES), 1)

    def head_body(hl, carry):
        hmask = (lane >= hl * HEAD_DIM) & (lane < (hl + 1) * HEAD_DIM)
        mvec16 = jnp.where(hmask, 1.0, 0.0).astype(_BF16)
        for qi in range(nb):
            r0 = qi * tq
            qh = q_ref[0, r0:r0 + tq, :] * mvec16
            cq2 = cc_ref[0, 0, r0:r0 + tq, :]
            cq = jnp.where(hl == 0, cq2[:, 0:1], cq2[:, 1:2])
            ck = cr_ref[0, 0, pl.ds(hl, 1), r0:r0 + tq]
            s = _nt(qh, k_ref[0, r0:r0 + tq, :]) + (cq - ck) + causal_ref[...]
            m, l, acc = _softmax_first(s, v_ref[0, r0:r0 + tq, :])
            for j in range(qi):
                ck = cr_ref[0, 0, pl.ds(hl, 1), j * tq:(j + 1) * tq]
                s = _nt(qh, k_ref[0, j * tq:(j + 1) * tq, :]) + (cq - ck)
                m, l, acc = _softmax_next(s, v_ref[0, j * tq:(j + 1) * tq, :], m, l, acc)
            _store_head(o_ref, r0, tq, hl, lane, (acc * (1.0 / l)).astype(o_ref.dtype))
        return carry

    lax.fori_loop(0, HEADS_PER_TILE, head_body, 0)


def _fox_call(qkv, cum_col, cum_row, n_tiles, q_off, k_off, v_off, tq):
    B, S, _ = qkv.shape
    blk = (1, S, LANES)
    t = jnp.arange(tq, dtype=jnp.int32)
    causal = jnp.where(t[:, None] >= t[None, :], 0.0, NEG).astype(_F32)
    return pl.pallas_call(
        _fox_kernel,
        grid=(B, n_tiles),
        in_specs=[pl.BlockSpec(blk, lambda b, t: (b, 0, q_off + t)),
                  pl.BlockSpec(blk, lambda b, t: (b, 0, k_off + t)),
                  pl.BlockSpec(blk, lambda b, t: (b, 0, v_off + t)),
                  pl.BlockSpec((1, 1, S, HEADS_PER_TILE), lambda b, t: (b, t, 0, 0)),
                  pl.BlockSpec((1, 1, HEADS_PER_TILE, S), lambda b, t: (b, t, 0, 0)),
                  pl.BlockSpec((tq, tq), lambda b, t: (0, 0))],
        out_specs=pl.BlockSpec(blk, lambda b, t: (b, 0, t)),
        out_shape=jax.ShapeDtypeStruct((B, S, n_tiles * LANES), _BF16),
        compiler_params=pltpu.CompilerParams(
            dimension_semantics=("parallel", "arbitrary"), vmem_limit_bytes=VMEM_LIMIT),
        name="fox_attention",
    )(qkv, qkv, qkv, cum_col, cum_row, causal)


def _rms(x, g):
    ms = jnp.mean(x * x, axis=-1, keepdims=True)
    return x * lax.rsqrt(ms + EPS) * g


def _mlp_kernel(x_ref, om_ref, of_ref, p_ref, wo_ref, gm_ref, wu_ref, wd_ref,
                gp_ref, wg_ref, wp_ref, out_ref, *, ff_chunk):
    wm = om_ref.shape[1]
    x1 = x_ref[...] + (_dot(om_ref[...], wo_ref[:wm, :]) + _dot(of_ref[...], wo_ref[wm:, :]))
    h = _rms(x1, gm_ref[...]).astype(_BF16)
    y = None
    for c in range(wu_ref.shape[1] // ff_chunk):
        u = _dot(h, wu_ref[:, c * ff_chunk:(c + 1) * ff_chunk])
        a = jnp.square(jnp.maximum(u, 0.0)).astype(_BF16)
        d = _dot(a, wd_ref[c * ff_chunk:(c + 1) * ff_chunk, :])
        y = d if y is None else y + d
    x2 = x1 + y
    hg = _rms(x2, gp_ref[...]).astype(_BF16)
    gate = jax.nn.sigmoid(_dot(hg, wg_ref[...]))
    out_ref[...] = x2 + gate * _dot(p_ref[...].astype(_BF16), wp_ref[...])


def _mlp_call(x2d, o_m, o_f, p2d, w_out, g_mlp, w_up, w_down, g_ple, w_gate, w_proj, tm):
    T, D = x2d.shape
    row = lambda w: pl.BlockSpec((tm, w), lambda i: (i, 0))
    const = lambda a: pl.BlockSpec(a.shape, lambda i: (0, 0), pipeline_mode=pl.Buffered(1))
    return pl.pallas_call(
        functools.partial(_mlp_kernel, ff_chunk=1024),
        grid=(T // tm,),
        in_specs=[row(D), row(o_m.shape[1]), row(o_f.shape[1]), row(p2d.shape[1]),
                  const(w_out), const(g_mlp), const(w_up), const(w_down),
                  const(g_ple), const(w_gate), const(w_proj)],
        out_specs=row(D),
        out_shape=jax.ShapeDtypeStruct((T, D), _F32),
        compiler_params=pltpu.CompilerParams(
            dimension_semantics=("parallel",), vmem_limit_bytes=VMEM_LIMIT),
        name="outproj_mlp_ple",
    )(x2d, o_m, o_f, p2d, w_out, g_mlp, w_up, w_down, g_ple, w_gate, w_proj)


def kernel(x, p, rel_bias, g_attn, w_in, b_f, gq_moba, gk_moba, gq_fox, gk_fox,
           w_out, g_mlp, w_up, w_down, g_ple, w_ple_gate, w_ple_proj):
    B, S, D = x.shape
    depth = p.shape[0]
    n_fox = b_f.shape[1]
    n_moba = rel_bias.shape[1]
    w_moba, w_fox = n_moba * HEAD_DIM, n_fox * HEAD_DIM
    w_main = 3 * w_moba + 3 * w_fox
    assert w_in.shape[2] == w_main + n_fox
    assert S % MOBA_BLOCK == 0 and w_moba % PROJ_CHUNK == 0 and w_fox % PROJ_CHUNK == 0
    assert n_fox <= LANES and n_fox % HEADS_PER_TILE == 0 and n_moba % HEADS_PER_TILE == 0

    assert MOBA_BLOCK + 1 >= MAX_DISTANCE
    rel_far = rel_bias[N_BUCKETS - 1]
    bias_tab = _bias_tables(rel_bias)

    cm, cf = w_moba // PROJ_CHUNK, w_fox // PROJ_CHUNK
    norm_chunks = (True,) * (2 * cm) + (False,) * cm + (True,) * (2 * cf) + (False,) * cf
    tiles_m, tiles_f = w_moba // LANES, w_fox // LANES
    ones_m, ones_f = jnp.ones((w_moba,), _F32), jnp.ones((w_fox,), _F32)

    for i in range(depth):
        gain_row = jnp.concatenate([
            jnp.tile(gq_moba[i] * SCALE, n_moba), jnp.tile(gk_moba[i], n_moba), ones_m,
            jnp.tile(gq_fox[i] * SCALE, n_fox), jnp.tile(gk_fox[i], n_fox), ones_f])[None, :]
        w_i = w_in[i]
        w_f = jnp.pad(w_i[:, w_main:], ((0, 0), (0, LANES - n_fox))).astype(_BF16)
        b_row = jnp.pad(b_f[i], (0, LANES - n_fox))[None, :]
        qkv, cum_c, cum_r = _proj_call(
            x, g_attn[i][None, :], w_i[:, :w_main].astype(_BF16), w_f, b_row, gain_row,
            norm_chunks, n_fox, tm=512)

        o_m = _moba_call(qkv, bias_tab, rel_far, tiles_m, 0, tiles_m, 2 * tiles_m)
        cum_col = cum_c.reshape(B, S, tiles_f, HEADS_PER_TILE).transpose(0, 2, 1, 3)
        cum_row = cum_r.reshape(B, tiles_f, HEADS_PER_TILE, S)
        f0 = 3 * tiles_m
        o_f = _fox_call(qkv, cum_col, cum_row, tiles_f, f0, f0 + tiles_f, f0 + 2 * tiles_f,
                        tq=256)

        x = _mlp_call(
            x.reshape(B * S, D), o_m.reshape(B * S, w_moba), o_f.reshape(B * S, w_fox),
            p[i].reshape(B * S, -1), w_out[i].astype(_BF16), g_mlp[i][None, :],
            w_up[i].astype(_BF16), w_down[i].astype(_BF16), g_ple[i][None, :],
            w_ple_gate[i].astype(_BF16), w_ple_proj[i].astype(_BF16), tm=512,
        ).reshape(B, S, D)
    return x
```

```python
import functools

import numpy as np
import jax
import jax.numpy as jnp
from jax import lax
from jax.experimental import pallas as pl
from jax.experimental.pallas import tpu as pltpu

HEAD_DIM = 64
MOBA_BLOCK = 256
MOBA_TOPK = 3
N_BUCKETS = 32
MAX_DISTANCE = 128
EPS = 1e-6
NEG = -1e30
SCALE = HEAD_DIM ** -0.5
LOG2E = 1.4426950408889634

LANES = 128
HEADS_PER_TILE = LANES // HEAD_DIM
PROJ_CHUNK = 512
VMEM_LIMIT = 56 * 1024 * 1024

_F32 = jnp.float32
_BF16 = jnp.bfloat16


def _dot(a, b):
    return jnp.dot(a, b, preferred_element_type=_F32)


def _t5_bucket(rel):
    rel = jnp.maximum(rel, 0)
    max_exact = N_BUCKETS // 2
    relf = jnp.maximum(rel, max_exact).astype(_F32)
    large = max_exact + (jnp.log(relf / max_exact) / np.log(MAX_DISTANCE / max_exact)
                         * (N_BUCKETS - max_exact)).astype(jnp.int32)
    large = jnp.minimum(large, N_BUCKETS - 1)
    return jnp.where(rel < max_exact, rel, large)


def _bias_kernel(rb_ref, bown_ref, bprev_ref, out_ref):
    h = pl.program_id(0)
    bown = bown_ref[...]
    bprev = bprev_ref[...]
    t_own = jnp.zeros(bown.shape, _F32)
    t_prev = jnp.zeros(bprev.shape, _F32)
    for b in range(N_BUCKETS):
        v = rb_ref[b, h]
        t_own = jnp.where(bown == b, v, t_own)
        t_prev = jnp.where(bprev == b, v, t_prev)
    out_ref[0, 0] = jnp.where(bown < 0, -NEG, t_own * -LOG2E)
    out_ref[0, 1] = t_prev * -LOG2E


def _bias_tables(rel_bias):
    n_heads = rel_bias.shape[1]
    t = jnp.arange(MOBA_BLOCK, dtype=jnp.int32)
    rel = t[None, :] - t[:, None]
    bown = jnp.where(rel >= 0, _t5_bucket(rel), -1).astype(jnp.int32)
    bprev = _t5_bucket(rel + MOBA_BLOCK).astype(jnp.int32)
    blk = (MOBA_BLOCK, MOBA_BLOCK)
    return pl.pallas_call(
        _bias_kernel,
        grid=(n_heads,),
        in_specs=[pl.BlockSpec(memory_space=pltpu.SMEM),
                  pl.BlockSpec(blk, lambda h: (0, 0)),
                  pl.BlockSpec(blk, lambda h: (0, 0))],
        out_specs=pl.BlockSpec((1, 2) + blk, lambda h: (h, 0, 0, 0)),
        out_shape=jax.ShapeDtypeStruct((n_heads, 2) + blk, _F32),
        name="moba_bias_tables",
    )(rel_bias, bown, bprev)


def _proj_kernel(x_ref, g_ref, w_ref, wf_ref, bf_ref, gain_ref, tri_ref,
                 qkv_ref, cumc_ref, cumr_ref, carry_ref, *, norm_chunks):
    tm = x_ref.shape[1]
    n_f = cumc_ref.shape[2]

    @pl.when(pl.program_id(1) == 0)
    def _():
        carry_ref[...] = jnp.zeros_like(carry_ref)

    x = x_ref[0]
    ms = jnp.mean(x * x, axis=-1, keepdims=True)
    h = (x * lax.rsqrt(ms + EPS) * g_ref[...]).astype(_BF16)

    lane = lax.broadcasted_iota(jnp.int32, (1, LANES), 1)
    lo_mask = lane < HEAD_DIM
    for c, do_norm in enumerate(norm_chunks):
        acc = _dot(h, w_ref[:, c * PROJ_CHUNK:(c + 1) * PROJ_CHUNK])
        for s in range(PROJ_CHUNK // LANES):
            c0 = c * PROJ_CHUNK + s * LANES
            y = acc[:, s * LANES:(s + 1) * LANES]
            if do_norm:
                sq = y * y
                s_lo = jnp.sum(jnp.where(lo_mask, sq, 0.0), axis=-1, keepdims=True)
                s_hi = jnp.sum(jnp.where(lo_mask, 0.0, sq), axis=-1, keepdims=True)
                msq = jnp.where(lo_mask, s_lo, s_hi) * (1.0 / HEAD_DIM)
                y = y * lax.rsqrt(msq + EPS) * gain_ref[:, c0:c0 + LANES]
            qkv_ref[0, :, c0:c0 + LANES] = y.astype(_BF16)

    z = _dot(h, wf_ref[...]) + bf_ref[...]
    lf = (jnp.minimum(z, 0.0) - jnp.log1p(jnp.exp(-jnp.abs(z)))) * LOG2E
    hi = lf.astype(_BF16)
    r1 = lf - hi.astype(_F32)
    mid = r1.astype(_BF16)
    lo = (r1 - mid.astype(_F32)).astype(_BF16)
    tri = tri_ref[...]
    cum = (_dot(tri, hi) + _dot(tri, mid)) + _dot(tri, lo) + carry_ref[...]
    carry_ref[...] = cum[tm - 1:tm, :]
    cumc_ref[0] = cum[:, :n_f]
    cumr_ref[0] = cum.T[:n_f, :]


def _proj_call(x, g_attn, w_main, w_f, b_f_row, gain_row, norm_chunks, n_fox, tm):
    B, S, D = x.shape
    W = w_main.shape[1]
    tri = jnp.tril(jnp.ones((tm, tm), _F32)).astype(_BF16)
    const = lambda shape: pl.BlockSpec(shape, lambda b, s: (0,) * len(shape))
    return pl.pallas_call(
        functools.partial(_proj_kernel, norm_chunks=norm_chunks),
        grid=(B, S // tm),
        in_specs=[pl.BlockSpec((1, tm, D), lambda b, s: (b, s, 0)),
                  const((1, D)), const((D, W)), const((D, LANES)), const((1, LANES)),
                  const((1, W)), const((tm, tm))],
        out_specs=[pl.BlockSpec((1, tm, W), lambda b, s: (b, s, 0)),
                   pl.BlockSpec((1, tm, n_fox), lambda b, s: (b, s, 0)),
                   pl.BlockSpec((1, n_fox, tm), lambda b, s: (b, 0, s))],
        out_shape=[jax.ShapeDtypeStruct((B, S, W), _BF16),
                   jax.ShapeDtypeStruct((B, S, n_fox), _F32),
                   jax.ShapeDtypeStruct((B, n_fox, S), _F32)],
        scratch_shapes=[pltpu.VMEM((1, LANES), _F32)],
        compiler_params=pltpu.CompilerParams(
            dimension_semantics=("parallel", "arbitrary"), vmem_limit_bytes=VMEM_LIMIT),
        name="proj_qknorm_cumsum",
    )(x, g_attn, w_main, w_f, b_f_row, gain_row, tri)


def _stage_heads(q_ref, v_ref, qT_scr, vT_scr):
    qT = q_ref[0].T
    vT = v_ref[0].T
    row = lax.broadcasted_iota(jnp.int32, (LANES, 1), 0)
    for h in range(HEADS_PER_TILE):
        mine = (row >= h * HEAD_DIM) & (row < (h + 1) * HEAD_DIM)
        qT_scr[h] = jnp.where(mine, qT, 0).astype(_BF16)
        vT_scr[h] = jnp.where(mine, vT, 1).astype(_BF16)


def _attend(nb, tq, k_ref, qT_scr, vT_scr, oT_scr, prep):
    heads = range(HEADS_PER_TILE)
    pairs = [(qi, j) for qi in range(nb) for j in [qi] + list(range(qi))]
    qk = lambda h, qi, j: _dot(k_ref[0, j * tq:(j + 1) * tq, :],
                               qT_scr[h, :, qi * tq:(qi + 1) * tq])

    def retire(h, pend, acc):
        pv, alpha, qi, last = pend
        acc = pv if alpha is None else alpha * acc + pv
        if last:
            other = (1 - h) * HEAD_DIM
            o = acc * (1.0 / acc[other:other + 1, :])
            oT_scr[h * HEAD_DIM:(h + 1) * HEAD_DIM, qi * tq:(qi + 1) * tq] = \
                o[h * HEAD_DIM:(h + 1) * HEAD_DIM, :]
        return acc

    nxt = [qk(h, *pairs[0]) for h in heads]
    pend = [None for _ in heads]
    acc = [None for _ in heads]
    m = [None for _ in heads]
    for idx, (qi, j) in enumerate(pairs):
        cur = nxt
        if idx + 1 < len(pairs):
            nxt = [qk(h, *pairs[idx + 1]) for h in heads]
        for h in heads:
            if j == qi:
                m[h] = None
            sT, r = prep(h, qi, j, cur[h])
            mx = jnp.max(sT, axis=0, keepdims=True)
            if r is not None:
                mx = mx + r
            m_new = mx if m[h] is None else jnp.maximum(m[h], mx)
            p = jnp.exp2(sT - (m_new if r is None else m_new - r))
            alpha = None if m[h] is None else jnp.exp2(m[h] - m_new)
            m[h] = m_new
            pv = _dot(vT_scr[h, :, j * tq:(j + 1) * tq], p.astype(_BF16))
            if pend[h] is not None:
                acc[h] = retire(h, pend[h], acc[h])
            pend[h] = (pv, alpha, qi, j == max(qi - 1, 0))
    for h in heads:
        retire(h, pend[h], acc[h])


def _attn_scratch(S):
    tiles = pltpu.VMEM((HEADS_PER_TILE, LANES, S), _BF16)
    return [tiles, tiles, pltpu.VMEM((LANES, S), _F32)]


def _moba_kernel(far_ref, q_ref, k_ref, v_ref, nbias_ref, o_ref, qT_scr, vT_scr, oT_scr):
    S = q_ref.shape[1]
    tq = MOBA_BLOCK
    nb = S // tq
    hp = pl.program_id(1)
    _stage_heads(q_ref, v_ref, qT_scr, vT_scr)

    km_rows = [jnp.mean(k_ref[0, j * tq:(j + 1) * tq, :].astype(_F32), axis=0, keepdims=True)
               for j in range(nb)]
    km_rows += [jnp.zeros((1, LANES), _F32)] * (-nb % 16)
    km = jnp.concatenate(km_rows, axis=0)
    km_hi = km.astype(_BF16)
    km_lo = (km - km_hi.astype(_F32)).astype(_BF16)
    row = lax.broadcasted_iota(jnp.int32, (km.shape[0], tq), 0)
    far = [far_ref[hp * HEADS_PER_TILE + h] for h in range(HEADS_PER_TILE)]
    keep = {}

    def prep(h, qi, j, sT):
        if j == qi:
            if qi > MOBA_TOPK:
                w = qT_scr[h, :, qi * tq:(qi + 1) * tq]
                ge = jnp.where(row < qi, _dot(km_hi, w) + _dot(km_lo, w), -jnp.inf)
                for jj in range(qi):
                    gj = ge[jj:jj + 1, :]
                    ahead = (ge > gj) | ((ge == gj) & (row < jj))
                    rank = jnp.sum(jnp.where(ahead, 1.0, 0.0), axis=0, keepdims=True)
                    keep[h, qi, jj] = jnp.where(rank < MOBA_TOPK, 0.0, NEG)
            return sT - nbias_ref[h, 0], None
        if j == qi - 1:
            return sT - nbias_ref[h, 1], keep.get((h, qi, j))
        return sT, (keep[h, qi, j] + far[h] if (h, qi, j) in keep else far[h])

    _attend(nb, tq, k_ref, qT_scr, vT_scr, oT_scr, prep)
    o_ref[0] = oT_scr[...].T.astype(o_ref.dtype)


def _moba_call(qkv, bias_tab, rel_far, n_tiles, q_off, k_off, v_off):
    B, S, _ = qkv.shape
    blk = (1, S, LANES)
    return pl.pallas_call(
        _moba_kernel,
        grid=(B, n_tiles),
        in_specs=[pl.BlockSpec(memory_space=pltpu.SMEM),
                  pl.BlockSpec(blk, lambda b, t: (b, 0, q_off + t)),
                  pl.BlockSpec(blk, lambda b, t: (b, 0, k_off + t)),
                  pl.BlockSpec(blk, lambda b, t: (b, 0, v_off + t)),
                  pl.BlockSpec((HEADS_PER_TILE, 2, MOBA_BLOCK, MOBA_BLOCK),
                               lambda b, t: (t, 0, 0, 0))],
        out_specs=pl.BlockSpec(blk, lambda b, t: (b, 0, t)),
        out_shape=jax.ShapeDtypeStruct((B, S, n_tiles * LANES), _BF16),
        scratch_shapes=_attn_scratch(S),
        compiler_params=pltpu.CompilerParams(
            dimension_semantics=("parallel", "arbitrary"), vmem_limit_bytes=VMEM_LIMIT),
        name="moba_attention",
    )(rel_far, qkv, qkv, qkv, bias_tab)


def _fox_kernel(q_ref, k_ref, v_ref, cc_ref, cr_ref, causal_ref, o_ref,
                qT_scr, vT_scr, oT_scr, ckb_scr):
    S = q_ref.shape[1]
    tq = causal_ref.shape[0]
    nb = S // tq
    _stage_heads(q_ref, v_ref, qT_scr, vT_scr)
    cc2 = cc_ref[0, 0]
    for h in range(HEADS_PER_TILE):
        ckb_scr[h] = jnp.broadcast_to(cc2[:, h:h + 1], (S, LANES))

    def prep(h, qi, j, sT):
        ckb = ckb_scr[h, j * tq:(j + 1) * tq, :]
        sT = sT - jnp.concatenate([ckb, ckb], axis=1)
        if j == qi:
            sT = sT + causal_ref[...]
        return sT, cr_ref[0, 0, h:h + 1, qi * tq:(qi + 1) * tq]

    _attend(nb, tq, k_ref, qT_scr, vT_scr, oT_scr, prep)
    o_ref[0] = oT_scr[...].T.astype(o_ref.dtype)


def _fox_call(qkv, cum_col, cum_row, n_tiles, q_off, k_off, v_off, tq):
    B, S, _ = qkv.shape
    blk = (1, S, LANES)
    t = jnp.arange(tq, dtype=jnp.int32)
    causal = jnp.where(t[:, None] <= t[None, :], 0.0, NEG).astype(_F32)
    return pl.pallas_call(
        _fox_kernel,
        grid=(B, n_tiles),
        in_specs=[pl.BlockSpec(blk, lambda b, t: (b, 0, q_off + t)),
                  pl.BlockSpec(blk, lambda b, t: (b, 0, k_off + t)),
                  pl.BlockSpec(blk, lambda b, t: (b, 0, v_off + t)),
                  pl.BlockSpec((1, 1, S, HEADS_PER_TILE), lambda b, t: (b, t, 0, 0)),
                  pl.BlockSpec((1, 1, HEADS_PER_TILE, S), lambda b, t: (b, t, 0, 0)),
                  pl.BlockSpec((tq, tq), lambda b, t: (0, 0))],
        out_specs=pl.BlockSpec(blk, lambda b, t: (b, 0, t)),
        out_shape=jax.ShapeDtypeStruct((B, S, n_tiles * LANES), _BF16),
        scratch_shapes=_attn_scratch(S) + [pltpu.VMEM((HEADS_PER_TILE, S, LANES), _F32)],
        compiler_params=pltpu.CompilerParams(
            dimension_semantics=("parallel", "arbitrary"), vmem_limit_bytes=VMEM_LIMIT),
        name="fox_attention",
    )(qkv, qkv, qkv, cum_col, cum_row, causal)


def _rms(x, g):
    ms = jnp.mean(x * x, axis=-1, keepdims=True)
    return x * lax.rsqrt(ms + EPS) * g


def _mlp_kernel(x_ref, om_ref, of_ref, p_ref, wo_ref, gm_ref, wu_ref, wd_ref,
                gp_ref, wg_ref, wp_ref, out_ref, *, ff_chunk):
    wm = om_ref.shape[1]
    x1 = x_ref[...] + (_dot(om_ref[...], wo_ref[:wm, :]) + _dot(of_ref[...], wo_ref[wm:, :]))
    h = _rms(x1, gm_ref[...]).astype(_BF16)
    y = None
    for c in range(wu_ref.shape[1] // ff_chunk):
        u = _dot(h, wu_ref[:, c * ff_chunk:(c + 1) * ff_chunk])
        a = jnp.square(jnp.maximum(u, 0.0)).astype(_BF16)
        d = _dot(a, wd_ref[c * ff_chunk:(c + 1) * ff_chunk, :])
        y = d if y is None else y + d
    x2 = x1 + y
    hg = _rms(x2, gp_ref[...]).astype(_BF16)
    gate = jax.nn.sigmoid(_dot(hg, wg_ref[...]))
    out_ref[...] = x2 + gate * _dot(p_ref[...].astype(_BF16), wp_ref[...])


def _mlp_call(x2d, o_m, o_f, p2d, w_out, g_mlp, w_up, w_down, g_ple, w_gate, w_proj, tm):
    T, D = x2d.shape
    row = lambda w: pl.BlockSpec((tm, w), lambda i: (i, 0))
    const = lambda a: pl.BlockSpec(a.shape, lambda i: (0, 0), pipeline_mode=pl.Buffered(1))
    return pl.pallas_call(
        functools.partial(_mlp_kernel, ff_chunk=1024),
        grid=(T // tm,),
        in_specs=[row(D), row(o_m.shape[1]), row(o_f.shape[1]), row(p2d.shape[1]),
                  const(w_out), const(g_mlp), const(w_up), const(w_down),
                  const(g_ple), const(w_gate), const(w_proj)],
        out_specs=row(D),
        out_shape=jax.ShapeDtypeStruct((T, D), _F32),
        compiler_params=pltpu.CompilerParams(
            dimension_semantics=("parallel",), vmem_limit_bytes=VMEM_LIMIT),
        name="outproj_mlp_ple",
    )(x2d, o_m, o_f, p2d, w_out, g_mlp, w_up, w_down, g_ple, w_gate, w_proj)


def kernel(x, p, rel_bias, g_attn, w_in, b_f, gq_moba, gk_moba, gq_fox, gk_fox,
           w_out, g_mlp, w_up, w_down, g_ple, w_ple_gate, w_ple_proj):
    B, S, D = x.shape
    depth = p.shape[0]
    n_fox = b_f.shape[1]
    n_moba = rel_bias.shape[1]
    w_moba, w_fox = n_moba * HEAD_DIM, n_fox * HEAD_DIM
    w_main = 3 * w_moba + 3 * w_fox
    assert w_in.shape[2] == w_main + n_fox
    assert S % MOBA_BLOCK == 0 and w_moba % PROJ_CHUNK == 0 and w_fox % PROJ_CHUNK == 0
    assert n_fox <= LANES and n_fox % HEADS_PER_TILE == 0 and n_moba % HEADS_PER_TILE == 0

    assert MOBA_BLOCK + 1 >= MAX_DISTANCE
    rel_far = rel_bias[N_BUCKETS - 1] * LOG2E
    bias_tab = _bias_tables(rel_bias)

    cm, cf = w_moba // PROJ_CHUNK, w_fox // PROJ_CHUNK
    norm_chunks = (True,) * (2 * cm) + (False,) * cm + (True,) * (2 * cf) + (False,) * cf
    tiles_m, tiles_f = w_moba // LANES, w_fox // LANES
    ones_m, ones_f = jnp.ones((w_moba,), _F32), jnp.ones((w_fox,), _F32)

    for i in range(depth):
        gain_row = jnp.concatenate([
            jnp.tile(gq_moba[i] * (SCALE * LOG2E), n_moba), jnp.tile(gk_moba[i], n_moba), ones_m,
            jnp.tile(gq_fox[i] * (SCALE * LOG2E), n_fox), jnp.tile(gk_fox[i], n_fox), ones_f])[None, :]
        w_i = w_in[i]
        w_f = jnp.pad(w_i[:, w_main:], ((0, 0), (0, LANES - n_fox))).astype(_BF16)
        b_row = jnp.pad(b_f[i], (0, LANES - n_fox))[None, :]
        qkv, cum_c, cum_r = _proj_call(
            x, g_attn[i][None, :], w_i[:, :w_main].astype(_BF16), w_f, b_row, gain_row,
            norm_chunks, n_fox, tm=512)

        o_m = _moba_call(qkv, bias_tab, rel_far, tiles_m, 0, tiles_m, 2 * tiles_m)
        cum_col = cum_c.reshape(B, S, tiles_f, HEADS_PER_TILE).transpose(0, 2, 1, 3)
        cum_row = cum_r.reshape(B, tiles_f, HEADS_PER_TILE, S)
        f0 = 3 * tiles_m
        o_f = _fox_call(qkv, cum_col, cum_row, tiles_f, f0, f0 + tiles_f, f0 + 2 * tiles_f,
                        tq=256)

        x = _mlp_call(
            x.reshape(B * S, D), o_m.reshape(B * S, w_moba), o_f.reshape(B * S, w_fox),
            p[i].reshape(B * S, -1), w_out[i].astype(_BF16), g_mlp[i][None, :],
            w_up[i].astype(_BF16), w_down[i].astype(_BF16), g_ple[i][None, :],
            w_ple_gate[i].astype(_BF16), w_ple_proj[i].astype(_BF16), tm=512,
        ).reshape(B, S, D)
    return x
```

```python
import functools

import numpy as np
import jax
import jax.numpy as jnp
from jax import lax
from jax.experimental import pallas as pl
from jax.experimental.pallas import tpu as pltpu

HEAD_DIM = 64
MOBA_BLOCK = 256
MOBA_TOPK = 3
N_BUCKETS = 32
MAX_DISTANCE = 128
EPS = 1e-6
NEG = -1e30
SCALE = HEAD_DIM ** -0.5
LOG2E = 1.4426950408889634

LANES = 128
HEADS_PER_TILE = LANES // HEAD_DIM
PROJ_CHUNK = 512
QK_AHEAD = 2
VMEM_LIMIT = 56 * 1024 * 1024

_F32 = jnp.float32
_BF16 = jnp.bfloat16


def _dot(a, b):
    return jnp.dot(a, b, preferred_element_type=_F32)


def _t5_bucket(rel):
    rel = jnp.maximum(rel, 0)
    max_exact = N_BUCKETS // 2
    relf = jnp.maximum(rel, max_exact).astype(_F32)
    large = max_exact + (jnp.log(relf / max_exact) / np.log(MAX_DISTANCE / max_exact)
                         * (N_BUCKETS - max_exact)).astype(jnp.int32)
    large = jnp.minimum(large, N_BUCKETS - 1)
    return jnp.where(rel < max_exact, rel, large)


def _bias_kernel(rb_ref, bown_ref, bprev_ref, out_ref):
    h = pl.program_id(0)
    bown = bown_ref[...]
    bprev = bprev_ref[...]
    t_own = jnp.zeros(bown.shape, _F32)
    t_prev = jnp.zeros(bprev.shape, _F32)
    for b in range(N_BUCKETS):
        v = rb_ref[b, h]
        t_own = jnp.where(bown == b, v, t_own)
        t_prev = jnp.where(bprev == b, v, t_prev)
    out_ref[0, 0] = jnp.where(bown < 0, -NEG, t_own * -LOG2E)
    out_ref[0, 1] = t_prev * -LOG2E


def _bias_tables(rel_bias):
    n_heads = rel_bias.shape[1]
    t = jnp.arange(MOBA_BLOCK, dtype=jnp.int32)
    rel = t[None, :] - t[:, None]
    bown = jnp.where(rel >= 0, _t5_bucket(rel), -1).astype(jnp.int32)
    bprev = _t5_bucket(rel + MOBA_BLOCK).astype(jnp.int32)
    blk = (MOBA_BLOCK, MOBA_BLOCK)
    return pl.pallas_call(
        _bias_kernel,
        grid=(n_heads,),
        in_specs=[pl.BlockSpec(memory_space=pltpu.SMEM),
                  pl.BlockSpec(blk, lambda h: (0, 0)),
                  pl.BlockSpec(blk, lambda h: (0, 0))],
        out_specs=pl.BlockSpec((1, 2) + blk, lambda h: (h, 0, 0, 0)),
        out_shape=jax.ShapeDtypeStruct((n_heads, 2) + blk, _F32),
        name="moba_bias_tables",
    )(rel_bias, bown, bprev)


def _proj_kernel(x_ref, g_ref, w_ref, wfT_ref, bf_ref, gain_ref, triu_ref,
                 qkv_ref, cum_ref, carry_ref, *, norm_chunks):
    tm = x_ref.shape[1]
    n_f = cum_ref.shape[1]

    @pl.when(pl.program_id(1) == 0)
    def _():
        carry_ref[...] = jnp.zeros_like(carry_ref)

    x = x_ref[0]
    ms = jnp.mean(x * x, axis=-1, keepdims=True)
    h = (x * lax.rsqrt(ms + EPS) * g_ref[...]).astype(_BF16)

    lane = lax.broadcasted_iota(jnp.int32, (1, LANES), 1)
    lo_mask = lane < HEAD_DIM
    for c, do_norm in enumerate(norm_chunks):
        acc = _dot(h, w_ref[:, c * PROJ_CHUNK:(c + 1) * PROJ_CHUNK])
        for s in range(PROJ_CHUNK // LANES):
            c0 = c * PROJ_CHUNK + s * LANES
            y = acc[:, s * LANES:(s + 1) * LANES]
            if do_norm:
                sq = y * y
                s_lo = jnp.sum(jnp.where(lo_mask, sq, 0.0), axis=-1, keepdims=True)
                s_hi = jnp.sum(jnp.where(lo_mask, 0.0, sq), axis=-1, keepdims=True)
                msq = jnp.where(lo_mask, s_lo, s_hi) * (1.0 / HEAD_DIM)
                y = y * lax.rsqrt(msq + EPS) * gain_ref[:, c0:c0 + LANES]
            qkv_ref[0, :, c0:c0 + LANES] = y.astype(_BF16)

    zT = lax.dot_general(wfT_ref[...], h, (((1,), (1,)), ((), ())),
                         preferred_element_type=_F32) + bf_ref[...]
    lfT = (jnp.minimum(zT, 0.0) - jnp.log1p(jnp.exp(-jnp.abs(zT)))) * LOG2E
    hi = lfT.astype(_BF16)
    r1 = lfT - hi.astype(_F32)
    mid = r1.astype(_BF16)
    lo = (r1 - mid.astype(_F32)).astype(_BF16)
    triu = triu_ref[...]
    cumT = carry_ref[:, 0:1] + ((_dot(hi, triu) + _dot(mid, triu)) + _dot(lo, triu))
    carry_ref[...] = jnp.broadcast_to(cumT[:, tm - 1:tm], carry_ref.shape)
    cum_ref[0] = cumT[:n_f, :]


def _proj_call(x, g_attn, w_bf16, wfT, b_f_col, gain_row, norm_chunks, n_fox, tm):
    B, S, D = x.shape
    W = gain_row.shape[1]
    rows = wfT.shape[0]
    triu = jnp.triu(jnp.ones((tm, tm), _F32)).astype(_BF16)
    const = lambda shape: pl.BlockSpec(shape, lambda b, s: (0,) * len(shape))
    return pl.pallas_call(
        functools.partial(_proj_kernel, norm_chunks=norm_chunks),
        grid=(B, S // tm),
        in_specs=[pl.BlockSpec((1, tm, D), lambda b, s: (b, s, 0)),
                  const((1, D)), const(w_bf16.shape), const((rows, D)), const((rows, 1)),
                  const((1, W)), const((tm, tm))],
        out_specs=[pl.BlockSpec((1, tm, W), lambda b, s: (b, s, 0)),
                   pl.BlockSpec((1, n_fox, tm), lambda b, s: (b, 0, s))],
        out_shape=[jax.ShapeDtypeStruct((B, S, W), _BF16),
                   jax.ShapeDtypeStruct((B, n_fox, S), _F32)],
        scratch_shapes=[pltpu.VMEM((rows, LANES), _F32)],
        compiler_params=pltpu.CompilerParams(
            dimension_semantics=("parallel", "arbitrary"), vmem_limit_bytes=VMEM_LIMIT),
        name="proj_qknorm_cumsum",
    )(x, g_attn, w_bf16, wfT, b_f_col, gain_row, triu)


def _stage_heads(q_ref, v_ref, qT_scr, vT_scr):
    qT = q_ref[0].T
    vT = v_ref[0].T
    row = lax.broadcasted_iota(jnp.int32, (LANES, 1), 0)
    for h in range(HEADS_PER_TILE):
        mine = (row >= h * HEAD_DIM) & (row < (h + 1) * HEAD_DIM)
        qT_scr[h] = jnp.where(mine, qT, 0).astype(_BF16)
        vT_scr[h] = jnp.where(mine, vT, 1).astype(_BF16)


def _attend(nb, tq, k_ref, qT_scr, vT_scr, oT_scr, prep):
    heads = range(HEADS_PER_TILE)
    pairs = [(qi, j) for qi in range(nb) for j in [qi] + list(range(qi))]
    qk = lambda qi, j: [_dot(k_ref[0, j * tq:(j + 1) * tq, :],
                             qT_scr[h, :, qi * tq:(qi + 1) * tq]) for h in heads]

    ahead = [qk(*pr) for pr in pairs[:QK_AHEAD]]
    acc = [None for _ in heads]
    m = [None for _ in heads]
    for idx, (qi, j) in enumerate(pairs):
        cur = ahead.pop(0)
        if idx + QK_AHEAD < len(pairs):
            ahead.append(qk(*pairs[idx + QK_AHEAD]))
        for h in heads:
            sT, r = prep(h, qi, j, cur[h])
            mx = jnp.max(sT, axis=0, keepdims=True)
            if r is not None:
                mx = mx + r
            m_new = mx if j == qi else jnp.maximum(m[h], mx)
            p = jnp.exp2(sT - (m_new if r is None else m_new - r))
            pv = _dot(vT_scr[h, :, j * tq:(j + 1) * tq], p.astype(_BF16))
            acc[h] = pv if j == qi else jnp.exp2(m[h] - m_new) * acc[h] + pv
            m[h] = m_new
            if j == max(qi - 1, 0):
                other = (1 - h) * HEAD_DIM
                o = acc[h] * (1.0 / acc[h][other:other + 1, :])
                oT_scr[h * HEAD_DIM:(h + 1) * HEAD_DIM, qi * tq:(qi + 1) * tq] = \
                    o[h * HEAD_DIM:(h + 1) * HEAD_DIM, :]


def _attn_scratch(S):
    tiles = pltpu.VMEM((HEADS_PER_TILE, LANES, S), _BF16)
    return [tiles, tiles, pltpu.VMEM((LANES, S), _F32)]


def _moba_kernel(far_ref, q_ref, k_ref, v_ref, nbias_ref, o_ref, qT_scr, vT_scr, oT_scr):
    S = q_ref.shape[1]
    tq = MOBA_BLOCK
    nb = S // tq
    hp = pl.program_id(1)
    _stage_heads(q_ref, v_ref, qT_scr, vT_scr)

    km_rows = [jnp.mean(k_ref[0, j * tq:(j + 1) * tq, :].astype(_F32), axis=0, keepdims=True)
               for j in range(nb)]
    km_rows += [jnp.zeros((1, LANES), _F32)] * (-nb % 16)
    km = jnp.concatenate(km_rows, axis=0)
    km_hi = km.astype(_BF16)
    km_lo = (km - km_hi.astype(_F32)).astype(_BF16)
    row = lax.broadcasted_iota(jnp.int32, (km.shape[0], tq), 0)
    far = [far_ref[hp * HEADS_PER_TILE + h] for h in range(HEADS_PER_TILE)]
    keep = {}

    def prep(h, qi, j, sT):
        if j == qi:
            if qi > MOBA_TOPK:
                w = qT_scr[h, :, qi * tq:(qi + 1) * tq]
                ge = jnp.where(row < qi, _dot(km_hi, w) + _dot(km_lo, w), -jnp.inf)
                for jj in range(qi):
                    gj = ge[jj:jj + 1, :]
                    ahead = (ge > gj) | ((ge == gj) & (row < jj))
                    rank = jnp.sum(jnp.where(ahead, 1.0, 0.0), axis=0, keepdims=True)
                    keep[h, qi, jj] = jnp.where(rank < MOBA_TOPK, 0.0, NEG)
            return sT - nbias_ref[h, 0], None
        if j == qi - 1:
            return sT - nbias_ref[h, 1], keep.get((h, qi, j))
        return sT, (keep[h, qi, j] + far[h] if (h, qi, j) in keep else far[h])

    _attend(nb, tq, k_ref, qT_scr, vT_scr, oT_scr, prep)
    o_ref[0] = oT_scr[...].T.astype(o_ref.dtype)


def _moba_call(qkv, bias_tab, rel_far, n_tiles, q_off, k_off, v_off):
    B, S, _ = qkv.shape
    blk = (1, S, LANES)
    return pl.pallas_call(
        _moba_kernel,
        grid=(B, n_tiles),
        in_specs=[pl.BlockSpec(memory_space=pltpu.SMEM),
                  pl.BlockSpec(blk, lambda b, t: (b, 0, q_off + t)),
                  pl.BlockSpec(blk, lambda b, t: (b, 0, k_off + t)),
                  pl.BlockSpec(blk, lambda b, t: (b, 0, v_off + t)),
                  pl.BlockSpec((HEADS_PER_TILE, 2, MOBA_BLOCK, MOBA_BLOCK),
                               lambda b, t: (t, 0, 0, 0))],
        out_specs=pl.BlockSpec(blk, lambda b, t: (b, 0, t)),
        out_shape=jax.ShapeDtypeStruct((B, S, n_tiles * LANES), _BF16),
        scratch_shapes=_attn_scratch(S),
        compiler_params=pltpu.CompilerParams(
            dimension_semantics=("parallel", "arbitrary"), vmem_limit_bytes=VMEM_LIMIT),
        name="moba_attention",
    )(rel_far, qkv, qkv, qkv, bias_tab)


def _fox_kernel(q_ref, k_ref, v_ref, cum_ref, causal_ref, o_ref,
                qT_scr, vT_scr, oT_scr, ckb_scr):
    S = q_ref.shape[1]
    tq = causal_ref.shape[0]
    nb = S // tq
    head0 = pl.program_id(1) * HEADS_PER_TILE
    _stage_heads(q_ref, v_ref, qT_scr, vT_scr)
    for h in range(HEADS_PER_TILE):
        ckb_scr[h] = jnp.broadcast_to(cum_ref[0, pl.ds(head0 + h, 1), :], (LANES, S)).T

    def prep(h, qi, j, sT):
        ckb = ckb_scr[h, j * tq:(j + 1) * tq, :]
        sT = sT - jnp.concatenate([ckb, ckb], axis=1)
        if j == qi:
            sT = sT + causal_ref[...]
        return sT, cum_ref[0, pl.ds(head0 + h, 1), qi * tq:(qi + 1) * tq]

    _attend(nb, tq, k_ref, qT_scr, vT_scr, oT_scr, prep)
    o_ref[0] = oT_scr[...].T.astype(o_ref.dtype)


def _fox_call(qkv, cum, n_tiles, q_off, k_off, v_off, tq):
    B, S, _ = qkv.shape
    blk = (1, S, LANES)
    t = jnp.arange(tq, dtype=jnp.int32)
    causal = jnp.where(t[:, None] <= t[None, :], 0.0, NEG).astype(_F32)
    return pl.pallas_call(
        _fox_kernel,
        grid=(B, n_tiles),
        in_specs=[pl.BlockSpec(blk, lambda b, t: (b, 0, q_off + t)),
                  pl.BlockSpec(blk, lambda b, t: (b, 0, k_off + t)),
                  pl.BlockSpec(blk, lambda b, t: (b, 0, v_off + t)),
                  pl.BlockSpec((1,) + cum.shape[1:], lambda b, t: (b, 0, 0)),
                  pl.BlockSpec((tq, tq), lambda b, t: (0, 0))],
        out_specs=pl.BlockSpec(blk, lambda b, t: (b, 0, t)),
        out_shape=jax.ShapeDtypeStruct((B, S, n_tiles * LANES), _BF16),
        scratch_shapes=_attn_scratch(S) + [pltpu.VMEM((HEADS_PER_TILE, S, LANES), _F32)],
        compiler_params=pltpu.CompilerParams(
            dimension_semantics=("parallel", "arbitrary"), vmem_limit_bytes=VMEM_LIMIT),
        name="fox_attention",
    )(qkv, qkv, qkv, cum, causal)


def _rms(x, g):
    ms = jnp.mean(x * x, axis=-1, keepdims=True)
    return x * lax.rsqrt(ms + EPS) * g


def _mlp_kernel(x_ref, om_ref, of_ref, p_ref, wo_ref, gm_ref, wu_ref, wd_ref,
                gp_ref, wg_ref, wp_ref, out_ref, *, ff_chunk):
    wm = om_ref.shape[1]
    x1 = x_ref[...] + (_dot(om_ref[...], wo_ref[:wm, :]) + _dot(of_ref[...], wo_ref[wm:, :]))
    h = _rms(x1, gm_ref[...]).astype(_BF16)
    y = None
    for c in range(wu_ref.shape[1] // ff_chunk):
        u = _dot(h, wu_ref[:, c * ff_chunk:(c + 1) * ff_chunk])
        a = jnp.square(jnp.maximum(u, 0.0)).astype(_BF16)
        d = _dot(a, wd_ref[c * ff_chunk:(c + 1) * ff_chunk, :])
        y = d if y is None else y + d
    x2 = x1 + y
    hg = _rms(x2, gp_ref[...]).astype(_BF16)
    gate = jax.nn.sigmoid(_dot(hg, wg_ref[...]))
    out_ref[...] = x2 + gate * _dot(p_ref[...].astype(_BF16), wp_ref[...])


def _mlp_call(x2d, o_m, o_f, p2d, w_out, g_mlp, w_up, w_down, g_ple, w_gate, w_proj, tm):
    T, D = x2d.shape
    row = lambda w: pl.BlockSpec((tm, w), lambda i: (i, 0))
    const = lambda a: pl.BlockSpec(a.shape, lambda i: (0, 0), pipeline_mode=pl.Buffered(1))
    return pl.pallas_call(
        functools.partial(_mlp_kernel, ff_chunk=1024),
        grid=(T // tm,),
        in_specs=[row(D), row(o_m.shape[1]), row(o_f.shape[1]), row(p2d.shape[1]),
                  const(w_out), const(g_mlp), const(w_up), const(w_down),
                  const(g_ple), const(w_gate), const(w_proj)],
        out_specs=row(D),
        out_shape=jax.ShapeDtypeStruct((T, D), _F32),
        compiler_params=pltpu.CompilerParams(
            dimension_semantics=("parallel",), vmem_limit_bytes=VMEM_LIMIT),
        name="outproj_mlp_ple",
    )(x2d, o_m, o_f, p2d, w_out, g_mlp, w_up, w_down, g_ple, w_gate, w_proj)


def kernel(x, p, rel_bias, g_attn, w_in, b_f, gq_moba, gk_moba, gq_fox, gk_fox,
           w_out, g_mlp, w_up, w_down, g_ple, w_ple_gate, w_ple_proj):
    B, S, D = x.shape
    depth = p.shape[0]
    n_fox = b_f.shape[1]
    n_moba = rel_bias.shape[1]
    w_moba, w_fox = n_moba * HEAD_DIM, n_fox * HEAD_DIM
    w_main = 3 * w_moba + 3 * w_fox
    assert w_in.shape[2] == w_main + n_fox
    assert S % MOBA_BLOCK == 0 and w_moba % PROJ_CHUNK == 0 and w_fox % PROJ_CHUNK == 0
    assert n_fox <= LANES and n_fox % HEADS_PER_TILE == 0 and n_moba % HEADS_PER_TILE == 0

    assert MOBA_BLOCK + 1 >= MAX_DISTANCE
    rel_far = rel_bias[N_BUCKETS - 1] * LOG2E
    bias_tab = _bias_tables(rel_bias)

    cm, cf = w_moba // PROJ_CHUNK, w_fox // PROJ_CHUNK
    norm_chunks = (True,) * (2 * cm) + (False,) * cm + (True,) * (2 * cf) + (False,) * cf
    tiles_m, tiles_f = w_moba // LANES, w_fox // LANES
    ones_m, ones_f = jnp.ones((w_moba,), _F32), jnp.ones((w_fox,), _F32)

    for i in range(depth):
        gain_row = jnp.concatenate([
            jnp.tile(gq_moba[i] * (SCALE * LOG2E), n_moba), jnp.tile(gk_moba[i], n_moba), ones_m,
            jnp.tile(gq_fox[i] * (SCALE * LOG2E), n_fox), jnp.tile(gk_fox[i], n_fox), ones_f])[None, :]
        w_i = w_in[i].astype(_BF16)
        pad_f = -n_fox % 16
        wfT = jnp.pad(w_i[:, w_main:].T, ((0, pad_f), (0, 0)))
        b_col = jnp.pad(b_f[i], (0, pad_f))[:, None]
        qkv, cum = _proj_call(x, g_attn[i][None, :], w_i, wfT, b_col, gain_row,
                              norm_chunks, n_fox, tm=512)

        o_m = _moba_call(qkv, bias_tab, rel_far, tiles_m, 0, tiles_m, 2 * tiles_m)
        f0 = 3 * tiles_m
        o_f = _fox_call(qkv, cum, tiles_f, f0, f0 + tiles_f, f0 + 2 * tiles_f, tq=256)

        x = _mlp_call(
            x.reshape(B * S, D), o_m.reshape(B * S, w_moba), o_f.reshape(B * S, w_fox),
            p[i].reshape(B * S, -1), w_out[i].astype(_BF16), g_mlp[i][None, :],
            w_up[i].astype(_BF16), w_down[i].astype(_BF16), g_ple[i][None, :],
            w_ple_gate[i].astype(_BF16), w_ple_proj[i].astype(_BF16), tm=512,
        ).reshape(B, S, D)
    return x
```

```python
import functools

import numpy as np
import jax
import jax.numpy as jnp
from jax import lax
from jax.experimental import pallas as pl
from jax.experimental.pallas import tpu as pltpu

HEAD_DIM = 64
MOBA_BLOCK = 256
MOBA_TOPK = 3
N_BUCKETS = 32
MAX_DISTANCE = 128
EPS = 1e-6
NEG = -1e30
SCALE = HEAD_DIM ** -0.5
LOG2E = 1.4426950408889634

LANES = 128
HEADS_PER_TILE = LANES // HEAD_DIM
PROJ_CHUNK = 512
QK_AHEAD = 2
ONES_ROWS = 16
VMEM_LIMIT = 56 * 1024 * 1024

_F32 = jnp.float32
_BF16 = jnp.bfloat16


def _dot(a, b):
    return jnp.dot(a, b, preferred_element_type=_F32)


def _t5_bucket(rel):
    rel = jnp.maximum(rel, 0)
    max_exact = N_BUCKETS // 2
    relf = jnp.maximum(rel, max_exact).astype(_F32)
    large = max_exact + (jnp.log(relf / max_exact) / np.log(MAX_DISTANCE / max_exact)
                         * (N_BUCKETS - max_exact)).astype(jnp.int32)
    large = jnp.minimum(large, N_BUCKETS - 1)
    return jnp.where(rel < max_exact, rel, large)


def _bias_kernel(rb_ref, bown_ref, bprev_ref, out_ref):
    h = pl.program_id(0)
    bown = bown_ref[...]
    bprev = bprev_ref[...]
    t_own = jnp.zeros(bown.shape, _F32)
    t_prev = jnp.zeros(bprev.shape, _F32)
    for b in range(N_BUCKETS):
        v = rb_ref[b, h]
        t_own = jnp.where(bown == b, v, t_own)
        t_prev = jnp.where(bprev == b, v, t_prev)
    out_ref[0, 0] = jnp.where(bown < 0, -NEG, t_own * -LOG2E)
    out_ref[0, 1] = t_prev * -LOG2E


def _bias_tables(rel_bias):
    n_heads = rel_bias.shape[1]
    t = jnp.arange(MOBA_BLOCK, dtype=jnp.int32)
    rel = t[None, :] - t[:, None]
    bown = jnp.where(rel >= 0, _t5_bucket(rel), -1).astype(jnp.int32)
    bprev = _t5_bucket(rel + MOBA_BLOCK).astype(jnp.int32)
    blk = (MOBA_BLOCK, MOBA_BLOCK)
    return pl.pallas_call(
        _bias_kernel,
        grid=(n_heads,),
        in_specs=[pl.BlockSpec(memory_space=pltpu.SMEM),
                  pl.BlockSpec(blk, lambda h: (0, 0)),
                  pl.BlockSpec(blk, lambda h: (0, 0))],
        out_specs=pl.BlockSpec((1, 2) + blk, lambda h: (h, 0, 0, 0)),
        out_shape=jax.ShapeDtypeStruct((n_heads, 2) + blk, _F32),
        name="moba_bias_tables",
    )(rel_bias, bown, bprev)


def _proj_kernel(x_ref, g_ref, w_ref, wfT_ref, bf_ref, gain_ref, triu_ref,
                 qkv_ref, cum_ref, carry_ref, *, norm_chunks):
    tm = x_ref.shape[1]
    n_f = cum_ref.shape[1]

    @pl.when(pl.program_id(1) == 0)
    def _():
        carry_ref[...] = jnp.zeros_like(carry_ref)

    x = x_ref[0]
    ms = jnp.mean(x * x, axis=-1, keepdims=True)
    h = (x * lax.rsqrt(ms + EPS) * g_ref[...]).astype(_BF16)

    lane = lax.broadcasted_iota(jnp.int32, (1, LANES), 1)
    lo_mask = lane < HEAD_DIM
    for c, do_norm in enumerate(norm_chunks):
        acc = _dot(h, w_ref[:, c * PROJ_CHUNK:(c + 1) * PROJ_CHUNK])
        for s in range(PROJ_CHUNK // LANES):
            c0 = c * PROJ_CHUNK + s * LANES
            y = acc[:, s * LANES:(s + 1) * LANES]
            if do_norm:
                sq = y * y
                s_lo = jnp.sum(jnp.where(lo_mask, sq, 0.0), axis=-1, keepdims=True)
                s_hi = jnp.sum(jnp.where(lo_mask, 0.0, sq), axis=-1, keepdims=True)
                msq = jnp.where(lo_mask, s_lo, s_hi) * (1.0 / HEAD_DIM)
                y = y * lax.rsqrt(msq + EPS) * gain_ref[:, c0:c0 + LANES]
            qkv_ref[0, :, c0:c0 + LANES] = y.astype(_BF16)

    zT = lax.dot_general(wfT_ref[...], h, (((1,), (1,)), ((), ())),
                         preferred_element_type=_F32) + bf_ref[...]
    lfT = (jnp.minimum(zT, 0.0) - jnp.log1p(jnp.exp(-jnp.abs(zT)))) * LOG2E
    hi = lfT.astype(_BF16)
    r1 = lfT - hi.astype(_F32)
    mid = r1.astype(_BF16)
    lo = (r1 - mid.astype(_F32)).astype(_BF16)
    triu = triu_ref[...]
    cumT = carry_ref[:, 0:1] + ((_dot(hi, triu) + _dot(mid, triu)) + _dot(lo, triu))
    carry_ref[...] = jnp.broadcast_to(cumT[:, tm - 1:tm], carry_ref.shape)
    cum_ref[0] = cumT[:n_f, :]


def _proj_call(x, g_attn, w_bf16, wfT, b_f_col, gain_row, norm_chunks, n_fox, tm):
    B, S, D = x.shape
    W = gain_row.shape[1]
    rows = wfT.shape[0]
    triu = jnp.triu(jnp.ones((tm, tm), _F32)).astype(_BF16)
    const = lambda shape: pl.BlockSpec(shape, lambda b, s: (0,) * len(shape))
    return pl.pallas_call(
        functools.partial(_proj_kernel, norm_chunks=norm_chunks),
        grid=(B, S // tm),
        in_specs=[pl.BlockSpec((1, tm, D), lambda b, s: (b, s, 0)),
                  const((1, D)), const(w_bf16.shape), const((rows, D)), const((rows, 1)),
                  const((1, W)), const((tm, tm))],
        out_specs=[pl.BlockSpec((1, tm, W), lambda b, s: (b, s, 0)),
                   pl.BlockSpec((1, n_fox, tm), lambda b, s: (b, 0, s))],
        out_shape=[jax.ShapeDtypeStruct((B, S, W), _BF16),
                   jax.ShapeDtypeStruct((B, n_fox, S), _F32)],
        scratch_shapes=[pltpu.VMEM((rows, LANES), _F32)],
        compiler_params=pltpu.CompilerParams(
            dimension_semantics=("parallel", "arbitrary"), vmem_limit_bytes=VMEM_LIMIT),
        name="proj_qknorm_cumsum",
    )(x, g_attn, w_bf16, wfT, b_f_col, gain_row, triu)


def _stage_heads(q_ref, v_ref, qT_scr, vT_scr):
    S = q_ref.shape[1]
    qT = q_ref[0].T
    vT = v_ref[0].T
    row = lax.broadcasted_iota(jnp.int32, (LANES, 1), 0)
    ones = jnp.ones((ONES_ROWS, S), _BF16)
    for h in range(HEADS_PER_TILE):
        mine = (row >= h * HEAD_DIM) & (row < (h + 1) * HEAD_DIM)
        qT_scr[h] = jnp.where(mine, qT, 0).astype(_BF16)
        vT_scr[h, :HEAD_DIM, :] = vT[h * HEAD_DIM:(h + 1) * HEAD_DIM, :]
        vT_scr[h, HEAD_DIM:, :] = ones


def _attend(nb, tq, k_ref, qT_scr, vT_scr, oT_scr, prep):
    heads = range(HEADS_PER_TILE)
    pairs = [(qi, j) for qi in range(nb) for j in [qi] + list(range(qi))]
    qk = lambda qi, j: [_dot(k_ref[0, j * tq:(j + 1) * tq, :],
                             qT_scr[h, :, qi * tq:(qi + 1) * tq]) for h in heads]

    ahead = [qk(*pr) for pr in pairs[:QK_AHEAD]]
    acc = [None for _ in heads]
    m = [None for _ in heads]
    for idx, (qi, j) in enumerate(pairs):
        cur = ahead.pop(0)
        if idx + QK_AHEAD < len(pairs):
            ahead.append(qk(*pairs[idx + QK_AHEAD]))
        for h in heads:
            sT, r = prep(h, qi, j, cur[h])
            mx = jnp.max(sT, axis=0, keepdims=True)
            if r is not None:
                mx = mx + r
            m_new = mx if j == qi else jnp.maximum(m[h], mx)
            p = jnp.exp2((sT - (m_new if r is None else m_new - r)).astype(_BF16))
            pv = _dot(vT_scr[h, :, j * tq:(j + 1) * tq], p)
            acc[h] = pv if j == qi else jnp.exp2(m[h] - m_new) * acc[h] + pv
            m[h] = m_new
            if j == max(qi - 1, 0):
                o = acc[h][:HEAD_DIM, :] * (1.0 / acc[h][HEAD_DIM:HEAD_DIM + 1, :])
                oT_scr[h * HEAD_DIM:(h + 1) * HEAD_DIM, qi * tq:(qi + 1) * tq] = o


def _attn_scratch(S):
    return [pltpu.VMEM((HEADS_PER_TILE, LANES, S), _BF16),
            pltpu.VMEM((HEADS_PER_TILE, HEAD_DIM + ONES_ROWS, S), _BF16),
            pltpu.VMEM((LANES, S), _F32)]


def _moba_kernel(far_ref, q_ref, k_ref, v_ref, nbias_ref, o_ref, qT_scr, vT_scr, oT_scr):
    S = q_ref.shape[1]
    tq = MOBA_BLOCK
    nb = S // tq
    hp = pl.program_id(1)
    _stage_heads(q_ref, v_ref, qT_scr, vT_scr)

    km_rows = [jnp.mean(k_ref[0, j * tq:(j + 1) * tq, :].astype(_F32), axis=0, keepdims=True)
               for j in range(nb)]
    km_rows += [jnp.zeros((1, LANES), _F32)] * (-nb % 16)
    km = jnp.concatenate(km_rows, axis=0)
    km_hi = km.astype(_BF16)
    km_lo = (km - km_hi.astype(_F32)).astype(_BF16)
    row = lax.broadcasted_iota(jnp.int32, (km.shape[0], tq), 0)
    far = [far_ref[hp * HEADS_PER_TILE + h] for h in range(HEADS_PER_TILE)]
    keep = {}

    def prep(h, qi, j, sT):
        if j == qi:
            if qi > MOBA_TOPK:
                w = qT_scr[h, :, qi * tq:(qi + 1) * tq]
                ge = jnp.where(row < qi, _dot(km_hi, w) + _dot(km_lo, w), -jnp.inf)
                for jj in range(qi):
                    gj = ge[jj:jj + 1, :]
                    ahead = (ge > gj) | ((ge == gj) & (row < jj))
                    rank = jnp.sum(jnp.where(ahead, 1.0, 0.0), axis=0, keepdims=True)
                    keep[h, qi, jj] = jnp.where(rank < MOBA_TOPK, 0.0, NEG)
            return sT - nbias_ref[h, 0], None
        if j == qi - 1:
            return sT - nbias_ref[h, 1], keep.get((h, qi, j))
        return sT, (keep[h, qi, j] + far[h] if (h, qi, j) in keep else far[h])

    _attend(nb, tq, k_ref, qT_scr, vT_scr, oT_scr, prep)
    o_ref[0] = oT_scr[...].T.astype(o_ref.dtype)


def _moba_call(qkv, bias_tab, rel_far, n_tiles, q_off, k_off, v_off):
    B, S, _ = qkv.shape
    blk = (1, S, LANES)
    return pl.pallas_call(
        _moba_kernel,
        grid=(B, n_tiles),
        in_specs=[pl.BlockSpec(memory_space=pltpu.SMEM),
                  pl.BlockSpec(blk, lambda b, t: (b, 0, q_off + t)),
                  pl.BlockSpec(blk, lambda b, t: (b, 0, k_off + t)),
                  pl.BlockSpec(blk, lambda b, t: (b, 0, v_off + t)),
                  pl.BlockSpec((HEADS_PER_TILE, 2, MOBA_BLOCK, MOBA_BLOCK),
                               lambda b, t: (t, 0, 0, 0))],
        out_specs=pl.BlockSpec(blk, lambda b, t: (b, 0, t)),
        out_shape=jax.ShapeDtypeStruct((B, S, n_tiles * LANES), _BF16),
        scratch_shapes=_attn_scratch(S),
        compiler_params=pltpu.CompilerParams(
            dimension_semantics=("parallel", "arbitrary"), vmem_limit_bytes=VMEM_LIMIT),
        name="moba_attention",
    )(rel_far, qkv, qkv, qkv, bias_tab)


def _fox_kernel(q_ref, k_ref, v_ref, cum_ref, causal_ref, o_ref,
                qT_scr, vT_scr, oT_scr, ckb_scr):
    S = q_ref.shape[1]
    tq = causal_ref.shape[0]
    nb = S // tq
    head0 = pl.program_id(1) * HEADS_PER_TILE
    _stage_heads(q_ref, v_ref, qT_scr, vT_scr)
    for h in range(HEADS_PER_TILE):
        ckb_scr[h] = jnp.broadcast_to(cum_ref[0, pl.ds(head0 + h, 1), :], (LANES, S)).T

    def prep(h, qi, j, sT):
        ckb = ckb_scr[h, j * tq:(j + 1) * tq, :]
        sT = sT - jnp.concatenate([ckb, ckb], axis=1)
        if j == qi:
            sT = sT + causal_ref[...]
        return sT, cum_ref[0, pl.ds(head0 + h, 1), qi * tq:(qi + 1) * tq]

    _attend(nb, tq, k_ref, qT_scr, vT_scr, oT_scr, prep)
    o_ref[0] = oT_scr[...].T.astype(o_ref.dtype)


def _fox_call(qkv, cum, n_tiles, q_off, k_off, v_off, tq):
    B, S, _ = qkv.shape
    blk = (1, S, LANES)
    t = jnp.arange(tq, dtype=jnp.int32)
    causal = jnp.where(t[:, None] <= t[None, :], 0.0, NEG).astype(_F32)
    return pl.pallas_call(
        _fox_kernel,
        grid=(B, n_tiles),
        in_specs=[pl.BlockSpec(blk, lambda b, t: (b, 0, q_off + t)),
                  pl.BlockSpec(blk, lambda b, t: (b, 0, k_off + t)),
                  pl.BlockSpec(blk, lambda b, t: (b, 0, v_off + t)),
                  pl.BlockSpec((1,) + cum.shape[1:], lambda b, t: (b, 0, 0)),
                  pl.BlockSpec((tq, tq), lambda b, t: (0, 0))],
        out_specs=pl.BlockSpec(blk, lambda b, t: (b, 0, t)),
        out_shape=jax.ShapeDtypeStruct((B, S, n_tiles * LANES), _BF16),
        scratch_shapes=_attn_scratch(S) + [pltpu.VMEM((HEADS_PER_TILE, S, LANES), _F32)],
        compiler_params=pltpu.CompilerParams(
            dimension_semantics=("parallel", "arbitrary"), vmem_limit_bytes=VMEM_LIMIT),
        name="fox_attention",
    )(qkv, qkv, qkv, cum, causal)


def _rms(x, g):
    ms = jnp.mean(x * x, axis=-1, keepdims=True)
    return x * lax.rsqrt(ms + EPS) * g


def _mlp_kernel(x_ref, om_ref, of_ref, p_ref, wo_ref, gm_ref, wu_ref, wd_ref,
                gp_ref, wg_ref, wp_ref, out_ref, *, ff_chunk):
    wm = om_ref.shape[1]
    x1 = x_ref[...] + (_dot(om_ref[...], wo_ref[:wm, :]) + _dot(of_ref[...], wo_ref[wm:, :]))
    h = _rms(x1, gm_ref[...]).astype(_BF16)
    y = None
    for c in range(wu_ref.shape[1] // ff_chunk):
        u = _dot(h, wu_ref[:, c * ff_chunk:(c + 1) * ff_chunk])
        a = jnp.square(jnp.maximum(u, 0.0)).astype(_BF16)
        d = _dot(a, wd_ref[c * ff_chunk:(c + 1) * ff_chunk, :])
        y = d if y is None else y + d
    x2 = x1 + y
    hg = _rms(x2, gp_ref[...]).astype(_BF16)
    gate = jax.nn.sigmoid(_dot(hg, wg_ref[...]))
    out_ref[...] = x2 + gate * _dot(p_ref[...].astype(_BF16), wp_ref[...])


def _mlp_call(x2d, o_m, o_f, p2d, w_out, g_mlp, w_up, w_down, g_ple, w_gate, w_proj, tm):
    T, D = x2d.shape
    row = lambda w: pl.BlockSpec((tm, w), lambda i: (i, 0))
    const = lambda a: pl.BlockSpec(a.shape, lambda i: (0, 0), pipeline_mode=pl.Buffered(1))
    return pl.pallas_call(
        functools.partial(_mlp_kernel, ff_chunk=1024),
        grid=(T // tm,),
        in_specs=[row(D), row(o_m.shape[1]), row(o_f.shape[1]), row(p2d.shape[1]),
                  const(w_out), const(g_mlp), const(w_up), const(w_down),
                  const(g_ple), const(w_gate), const(w_proj)],
        out_specs=row(D),
        out_shape=jax.ShapeDtypeStruct((T, D), _F32),
        compiler_params=pltpu.CompilerParams(
            dimension_semantics=("parallel",), vmem_limit_bytes=VMEM_LIMIT),
        name="outproj_mlp_ple",
    )(x2d, o_m, o_f, p2d, w_out, g_mlp, w_up, w_down, g_ple, w_gate, w_proj)


def kernel(x, p, rel_bias, g_attn, w_in, b_f, gq_moba, gk_moba, gq_fox, gk_fox,
           w_out, g_mlp, w_up, w_down, g_ple, w_ple_gate, w_ple_proj):
    B, S, D = x.shape
    depth = p.shape[0]
    n_fox = b_f.shape[1]
    n_moba = rel_bias.shape[1]
    w_moba, w_fox = n_moba * HEAD_DIM, n_fox * HEAD_DIM
    w_main = 3 * w_moba + 3 * w_fox
    assert w_in.shape[2] == w_main + n_fox
    assert S % MOBA_BLOCK == 0 and w_moba % PROJ_CHUNK == 0 and w_fox % PROJ_CHUNK == 0
    assert n_fox <= LANES and n_fox % HEADS_PER_TILE == 0 and n_moba % HEADS_PER_TILE == 0

    assert MOBA_BLOCK + 1 >= MAX_DISTANCE
    rel_far = rel_bias[N_BUCKETS - 1] * LOG2E
    bias_tab = _bias_tables(rel_bias)

    cm, cf = w_moba // PROJ_CHUNK, w_fox // PROJ_CHUNK
    norm_chunks = (True,) * (2 * cm) + (False,) * cm + (True,) * (2 * cf) + (False,) * cf
    tiles_m, tiles_f = w_moba // LANES, w_fox // LANES
    ones_m, ones_f = jnp.ones((w_moba,), _F32), jnp.ones((w_fox,), _F32)

    for i in range(depth):
        gain_row = jnp.concatenate([
            jnp.tile(gq_moba[i] * (SCALE * LOG2E), n_moba), jnp.tile(gk_moba[i], n_moba), ones_m,
            jnp.tile(gq_fox[i] * (SCALE * LOG2E), n_fox), jnp.tile(gk_fox[i], n_fox), ones_f])[None, :]
        w_i = w_in[i].astype(_BF16)
        pad_f = -n_fox % 16
        wfT = jnp.pad(w_i[:, w_main:].T, ((0, pad_f), (0, 0)))
        b_col = jnp.pad(b_f[i], (0, pad_f))[:, None]
        qkv, cum = _proj_call(x, g_attn[i][None, :], w_i, wfT, b_col, gain_row,
                              norm_chunks, n_fox, tm=512)

        o_m = _moba_call(qkv, bias_tab, rel_far, tiles_m, 0, tiles_m, 2 * tiles_m)
        f0 = 3 * tiles_m
        o_f = _fox_call(qkv, cum, tiles_f, f0, f0 + tiles_f, f0 + 2 * tiles_f, tq=256)

        x = _mlp_call(
            x.reshape(B * S, D), o_m.reshape(B * S, w_moba), o_f.reshape(B * S, w_fox),
            p[i].reshape(B * S, -1), w_out[i].astype(_BF16), g_mlp[i][None, :],
            w_up[i].astype(_BF16), w_down[i].astype(_BF16), g_ple[i][None, :],
            w_ple_gate[i].astype(_BF16), w_ple_proj[i].astype(_BF16), tm=512,
        ).reshape(B, S, D)
    return x
```

```python
import functools

import numpy as np
import jax
import jax.numpy as jnp
from jax import lax
from jax.experimental import pallas as pl
from jax.experimental.pallas import tpu as pltpu

HEAD_DIM = 64
MOBA_BLOCK = 256
MOBA_TOPK = 3
N_BUCKETS = 32
MAX_DISTANCE = 128
EPS = 1e-6
NEG = -1e30
SCALE = HEAD_DIM ** -0.5
LOG2E = 1.4426950408889634

LANES = 128
HEADS_PER_TILE = LANES // HEAD_DIM
PROJ_CHUNK = 512
QK_AHEAD = 2
ONES_ROWS = 16
VMEM_LIMIT = 56 * 1024 * 1024

_F32 = jnp.float32
_BF16 = jnp.bfloat16


def _dot(a, b):
    return jnp.dot(a, b, preferred_element_type=_F32)


def _t5_bucket(rel):
    rel = jnp.maximum(rel, 0)
    max_exact = N_BUCKETS // 2
    relf = jnp.maximum(rel, max_exact).astype(_F32)
    large = max_exact + (jnp.log(relf / max_exact) / np.log(MAX_DISTANCE / max_exact)
                         * (N_BUCKETS - max_exact)).astype(jnp.int32)
    large = jnp.minimum(large, N_BUCKETS - 1)
    return jnp.where(rel < max_exact, rel, large)


def _bias_kernel(rb_ref, bown_ref, bprev_ref, out_ref):
    h = pl.program_id(0)
    bown = bown_ref[...]
    bprev = bprev_ref[...]
    t_own = jnp.zeros(bown.shape, _F32)
    t_prev = jnp.zeros(bprev.shape, _F32)
    for b in range(N_BUCKETS):
        v = rb_ref[b, h]
        t_own = jnp.where(bown == b, v, t_own)
        t_prev = jnp.where(bprev == b, v, t_prev)
    out_ref[0, 0] = jnp.where(bown < 0, -NEG, t_own * -LOG2E)
    out_ref[0, 1] = t_prev * -LOG2E


def _bias_tables(rel_bias):
    n_heads = rel_bias.shape[1]
    t = jnp.arange(MOBA_BLOCK, dtype=jnp.int32)
    rel = t[None, :] - t[:, None]
    bown = jnp.where(rel >= 0, _t5_bucket(rel), -1).astype(jnp.int32)
    bprev = _t5_bucket(rel + MOBA_BLOCK).astype(jnp.int32)
    blk = (MOBA_BLOCK, MOBA_BLOCK)
    return pl.pallas_call(
        _bias_kernel,
        grid=(n_heads,),
        in_specs=[pl.BlockSpec(memory_space=pltpu.SMEM),
                  pl.BlockSpec(blk, lambda h: (0, 0)),
                  pl.BlockSpec(blk, lambda h: (0, 0))],
        out_specs=pl.BlockSpec((1, 2) + blk, lambda h: (h, 0, 0, 0)),
        out_shape=jax.ShapeDtypeStruct((n_heads, 2) + blk, _F32),
        name="moba_bias_tables",
    )(rel_bias, bown, bprev)


def _proj_kernel(x_ref, g_ref, w_ref, wfT_ref, bf_ref, gain_ref, triu_ref,
                 qkv_ref, cum_ref, carry_ref, *, norm_chunks):
    tm = x_ref.shape[1]
    n_f = cum_ref.shape[1]

    @pl.when(pl.program_id(1) == 0)
    def _():
        carry_ref[...] = jnp.zeros_like(carry_ref)

    x = x_ref[0]
    ms = jnp.mean(x * x, axis=-1, keepdims=True)
    h = (x * lax.rsqrt(ms + EPS) * g_ref[...]).astype(_BF16)

    lane = lax.broadcasted_iota(jnp.int32, (1, LANES), 1)
    lo_mask = lane < HEAD_DIM
    for c, do_norm in enumerate(norm_chunks):
        acc = _dot(h, w_ref[:, c * PROJ_CHUNK:(c + 1) * PROJ_CHUNK])
        for s in range(PROJ_CHUNK // LANES):
            c0 = c * PROJ_CHUNK + s * LANES
            y = acc[:, s * LANES:(s + 1) * LANES]
            if do_norm:
                sq = y * y
                s_lo = jnp.sum(jnp.where(lo_mask, sq, 0.0), axis=-1, keepdims=True)
                s_hi = jnp.sum(jnp.where(lo_mask, 0.0, sq), axis=-1, keepdims=True)
                msq = jnp.where(lo_mask, s_lo, s_hi) * (1.0 / HEAD_DIM)
                y = y * lax.rsqrt(msq + EPS) * gain_ref[:, c0:c0 + LANES]
            qkv_ref[0, :, c0:c0 + LANES] = y.astype(_BF16)

    zT = lax.dot_general(wfT_ref[...], h, (((1,), (1,)), ((), ())),
                         preferred_element_type=_F32) + bf_ref[...]
    lfT = (jnp.minimum(zT, 0.0) - jnp.log1p(jnp.exp(-jnp.abs(zT)))) * LOG2E
    hi = lfT.astype(_BF16)
    r1 = lfT - hi.astype(_F32)
    mid = r1.astype(_BF16)
    lo = (r1 - mid.astype(_F32)).astype(_BF16)
    triu = triu_ref[...]
    cumT = carry_ref[:, 0:1] + ((_dot(hi, triu) + _dot(mid, triu)) + _dot(lo, triu))
    carry_ref[...] = jnp.broadcast_to(cumT[:, tm - 1:tm], carry_ref.shape)
    cum_ref[0] = cumT[:n_f, :]


def _proj_call(x, g_attn, w_bf16, wfT, b_f_col, gain_row, norm_chunks, n_fox, tm):
    B, S, D = x.shape
    W = gain_row.shape[1]
    rows = wfT.shape[0]
    triu = jnp.triu(jnp.ones((tm, tm), _F32)).astype(_BF16)
    const = lambda shape: pl.BlockSpec(shape, lambda b, s: (0,) * len(shape))
    return pl.pallas_call(
        functools.partial(_proj_kernel, norm_chunks=norm_chunks),
        grid=(B, S // tm),
        in_specs=[pl.BlockSpec((1, tm, D), lambda b, s: (b, s, 0)),
                  const((1, D)), const(w_bf16.shape), const((rows, D)), const((rows, 1)),
                  const((1, W)), const((tm, tm))],
        out_specs=[pl.BlockSpec((1, tm, W), lambda b, s: (b, s, 0)),
                   pl.BlockSpec((1, n_fox, tm), lambda b, s: (b, 0, s))],
        out_shape=[jax.ShapeDtypeStruct((B, S, W), _BF16),
                   jax.ShapeDtypeStruct((B, n_fox, S), _F32)],
        scratch_shapes=[pltpu.VMEM((rows, LANES), _F32)],
        compiler_params=pltpu.CompilerParams(
            dimension_semantics=("parallel", "arbitrary"), vmem_limit_bytes=VMEM_LIMIT),
        name="proj_qknorm_cumsum",
    )(x, g_attn, w_bf16, wfT, b_f_col, gain_row, triu)


def _stage_heads(q_ref, v_ref, qT_scr, vT_scr):
    S = q_ref.shape[1]
    qT = q_ref[0].T
    vT = v_ref[0].T
    row = lax.broadcasted_iota(jnp.int32, (LANES, 1), 0)
    ones = jnp.ones((ONES_ROWS, S), _BF16)
    for h in range(HEADS_PER_TILE):
        mine = (row >= h * HEAD_DIM) & (row < (h + 1) * HEAD_DIM)
        qT_scr[h] = jnp.where(mine, qT, 0).astype(_BF16)
        vT_scr[h, :HEAD_DIM, :] = vT[h * HEAD_DIM:(h + 1) * HEAD_DIM, :]
        vT_scr[h, HEAD_DIM:, :] = ones


def _attend(nb, tq, k_ref, qT_scr, vT_scr, oT_scr, prep):
    heads = range(HEADS_PER_TILE)
    th = tq // 2
    pairs = [(qi, j) for qi in range(nb) for j in [qi] + list(range(qi))]
    qk = lambda qi, j: [[_dot(k_ref[0, j * tq + u * th:j * tq + (u + 1) * th, :],
                              qT_scr[h, :, qi * tq:(qi + 1) * tq]) for u in range(2)]
                        for h in heads]

    ahead = [qk(*pr) for pr in pairs[:QK_AHEAD]]
    acc = [None for _ in heads]
    m = [None for _ in heads]
    for idx, (qi, j) in enumerate(pairs):
        cur = ahead.pop(0)
        if idx + QK_AHEAD < len(pairs):
            ahead.append(qk(*pairs[idx + QK_AHEAD]))
        for h in heads:
            run = None if j == qi else m[h]
            ps, shifts = [], []
            for u in range(2):
                sT, r = prep(h, qi, j, u, cur[h][u])
                mx = jnp.max(sT, axis=0, keepdims=True)
                if r is not None:
                    mx = mx + r
                run = mx if run is None else jnp.maximum(run, mx)
                ps.append(jnp.exp2((sT - (run if r is None else run - r)).astype(_BF16)))
                shifts.append(run)
            ps[0] = ps[0] * jnp.exp2(shifts[0] - shifts[1]).astype(_BF16)
            pv = _dot(vT_scr[h, :, j * tq:(j + 1) * tq], jnp.concatenate(ps, axis=0))
            acc[h] = pv if j == qi else jnp.exp2(m[h] - run) * acc[h] + pv
            m[h] = run
            if j == max(qi - 1, 0):
                o = acc[h][:HEAD_DIM, :] * (1.0 / acc[h][HEAD_DIM:HEAD_DIM + 1, :])
                oT_scr[h * HEAD_DIM:(h + 1) * HEAD_DIM, qi * tq:(qi + 1) * tq] = o


def _attn_scratch(S):
    return [pltpu.VMEM((HEADS_PER_TILE, LANES, S), _BF16),
            pltpu.VMEM((HEADS_PER_TILE, HEAD_DIM + ONES_ROWS, S), _BF16),
            pltpu.VMEM((LANES, S), _F32)]


def _moba_kernel(far_ref, q_ref, k_ref, v_ref, nbias_ref, o_ref, qT_scr, vT_scr, oT_scr):
    S = q_ref.shape[1]
    tq = MOBA_BLOCK
    nb = S // tq
    hp = pl.program_id(1)
    _stage_heads(q_ref, v_ref, qT_scr, vT_scr)

    km_rows = [jnp.mean(k_ref[0, j * tq:(j + 1) * tq, :].astype(_F32), axis=0, keepdims=True)
               for j in range(nb)]
    km_rows += [jnp.zeros((1, LANES), _F32)] * (-nb % 16)
    km = jnp.concatenate(km_rows, axis=0)
    km_hi = km.astype(_BF16)
    km_lo = (km - km_hi.astype(_F32)).astype(_BF16)
    row = lax.broadcasted_iota(jnp.int32, (km.shape[0], tq), 0)
    far = [far_ref[hp * HEADS_PER_TILE + h] for h in range(HEADS_PER_TILE)]
    keep = {}

    def prep(h, qi, j, u, sT):
        th = tq // 2
        if j == qi:
            if qi > MOBA_TOPK and u == 0:
                w = qT_scr[h, :, qi * tq:(qi + 1) * tq]
                ge = jnp.where(row < qi, _dot(km_hi, w) + _dot(km_lo, w), -jnp.inf)
                for jj in range(qi):
                    gj = ge[jj:jj + 1, :]
                    ahead = (ge > gj) | ((ge == gj) & (row < jj))
                    rank = jnp.sum(jnp.where(ahead, 1.0, 0.0), axis=0, keepdims=True)
                    keep[h, qi, jj] = jnp.where(rank < MOBA_TOPK, 0.0, NEG)
            return sT - nbias_ref[h, 0, u * th:(u + 1) * th, :], None
        if j == qi - 1:
            return sT - nbias_ref[h, 1, u * th:(u + 1) * th, :], keep.get((h, qi, j))
        return sT, (keep[h, qi, j] + far[h] if (h, qi, j) in keep else far[h])

    _attend(nb, tq, k_ref, qT_scr, vT_scr, oT_scr, prep)
    o_ref[0] = oT_scr[...].T.astype(o_ref.dtype)


def _moba_call(qkv, bias_tab, rel_far, n_tiles, q_off, k_off, v_off):
    B, S, _ = qkv.shape
    blk = (1, S, LANES)
    return pl.pallas_call(
        _moba_kernel,
        grid=(B, n_tiles),
        in_specs=[pl.BlockSpec(memory_space=pltpu.SMEM),
                  pl.BlockSpec(blk, lambda b, t: (b, 0, q_off + t)),
                  pl.BlockSpec(blk, lambda b, t: (b, 0, k_off + t)),
                  pl.BlockSpec(blk, lambda b, t: (b, 0, v_off + t)),
                  pl.BlockSpec((HEADS_PER_TILE, 2, MOBA_BLOCK, MOBA_BLOCK),
                               lambda b, t: (t, 0, 0, 0))],
        out_specs=pl.BlockSpec(blk, lambda b, t: (b, 0, t)),
        out_shape=jax.ShapeDtypeStruct((B, S, n_tiles * LANES), _BF16),
        scratch_shapes=_attn_scratch(S),
        compiler_params=pltpu.CompilerParams(
            dimension_semantics=("parallel", "arbitrary"), vmem_limit_bytes=VMEM_LIMIT),
        name="moba_attention",
    )(rel_far, qkv, qkv, qkv, bias_tab)


def _fox_kernel(q_ref, k_ref, v_ref, cum_ref, causal_ref, o_ref,
                qT_scr, vT_scr, oT_scr, ckb_scr):
    S = q_ref.shape[1]
    tq = causal_ref.shape[0]
    nb = S // tq
    head0 = pl.program_id(1) * HEADS_PER_TILE
    _stage_heads(q_ref, v_ref, qT_scr, vT_scr)
    for h in range(HEADS_PER_TILE):
        ckb_scr[h] = jnp.broadcast_to(cum_ref[0, pl.ds(head0 + h, 1), :], (LANES, S)).T

    def prep(h, qi, j, u, sT):
        th = tq // 2
        ckb = ckb_scr[h, j * tq + u * th:j * tq + (u + 1) * th, :]
        sT = sT - jnp.concatenate([ckb, ckb], axis=1)
        if j == qi:
            sT = sT + causal_ref[u * th:(u + 1) * th, :]
        return sT, cum_ref[0, pl.ds(head0 + h, 1), qi * tq:(qi + 1) * tq]

    _attend(nb, tq, k_ref, qT_scr, vT_scr, oT_scr, prep)
    o_ref[0] = oT_scr[...].T.astype(o_ref.dtype)


def _fox_call(qkv, cum, n_tiles, q_off, k_off, v_off, tq):
    B, S, _ = qkv.shape
    blk = (1, S, LANES)
    t = jnp.arange(tq, dtype=jnp.int32)
    causal = jnp.where(t[:, None] <= t[None, :], 0.0, NEG).astype(_F32)
    return pl.pallas_call(
        _fox_kernel,
        grid=(B, n_tiles),
        in_specs=[pl.BlockSpec(blk, lambda b, t: (b, 0, q_off + t)),
                  pl.BlockSpec(blk, lambda b, t: (b, 0, k_off + t)),
                  pl.BlockSpec(blk, lambda b, t: (b, 0, v_off + t)),
                  pl.BlockSpec((1,) + cum.shape[1:], lambda b, t: (b, 0, 0)),
                  pl.BlockSpec((tq, tq), lambda b, t: (0, 0))],
        out_specs=pl.BlockSpec(blk, lambda b, t: (b, 0, t)),
        out_shape=jax.ShapeDtypeStruct((B, S, n_tiles * LANES), _BF16),
        scratch_shapes=_attn_scratch(S) + [pltpu.VMEM((HEADS_PER_TILE, S, LANES), _F32)],
        compiler_params=pltpu.CompilerParams(
            dimension_semantics=("parallel", "arbitrary"), vmem_limit_bytes=VMEM_LIMIT),
        name="fox_attention",
    )(qkv, qkv, qkv, cum, causal)


def _rms(x, g):
    ms = jnp.mean(x * x, axis=-1, keepdims=True)
    return x * lax.rsqrt(ms + EPS) * g


def _mlp_kernel(x_ref, om_ref, of_ref, p_ref, wo_ref, gm_ref, wu_ref, wd_ref,
                gp_ref, wg_ref, wp_ref, out_ref, *, ff_chunk):
    wm = om_ref.shape[1]
    x1 = x_ref[...] + (_dot(om_ref[...], wo_ref[:wm, :]) + _dot(of_ref[...], wo_ref[wm:, :]))
    h = _rms(x1, gm_ref[...]).astype(_BF16)
    y = None
    for c in range(wu_ref.shape[1] // ff_chunk):
        u = _dot(h, wu_ref[:, c * ff_chunk:(c + 1) * ff_chunk])
        a = jnp.square(jnp.maximum(u, 0.0)).astype(_BF16)
        d = _dot(a, wd_ref[c * ff_chunk:(c + 1) * ff_chunk, :])
        y = d if y is None else y + d
    x2 = x1 + y
    hg = _rms(x2, gp_ref[...]).astype(_BF16)
    gate = jax.nn.sigmoid(_dot(hg, wg_ref[...]))
    out_ref[...] = x2 + gate * _dot(p_ref[...].astype(_BF16), wp_ref[...])


def _mlp_call(x2d, o_m, o_f, p2d, w_out, g_mlp, w_up, w_down, g_ple, w_gate, w_proj, tm):
    T, D = x2d.shape
    row = lambda w: pl.BlockSpec((tm, w), lambda i: (i, 0))
    const = lambda a: pl.BlockSpec(a.shape, lambda i: (0, 0), pipeline_mode=pl.Buffered(1))
    return pl.pallas_call(
        functools.partial(_mlp_kernel, ff_chunk=1024),
        grid=(T // tm,),
        in_specs=[row(D), row(o_m.shape[1]), row(o_f.shape[1]), row(p2d.shape[1]),
                  const(w_out), const(g_mlp), const(w_up), const(w_down),
                  const(g_ple), const(w_gate), const(w_proj)],
        out_specs=row(D),
        out_shape=jax.ShapeDtypeStruct((T, D), _F32),
        compiler_params=pltpu.CompilerParams(
            dimension_semantics=("parallel",), vmem_limit_bytes=VMEM_LIMIT),
        name="outproj_mlp_ple",
    )(x2d, o_m, o_f, p2d, w_out, g_mlp, w_up, w_down, g_ple, w_gate, w_proj)


def kernel(x, p, rel_bias, g_attn, w_in, b_f, gq_moba, gk_moba, gq_fox, gk_fox,
           w_out, g_mlp, w_up, w_down, g_ple, w_ple_gate, w_ple_proj):
    B, S, D = x.shape
    depth = p.shape[0]
    n_fox = b_f.shape[1]
    n_moba = rel_bias.shape[1]
    w_moba, w_fox = n_moba * HEAD_DIM, n_fox * HEAD_DIM
    w_main = 3 * w_moba + 3 * w_fox
    assert w_in.shape[2] == w_main + n_fox
    assert S % MOBA_BLOCK == 0 and w_moba % PROJ_CHUNK == 0 and w_fox % PROJ_CHUNK == 0
    assert n_fox <= LANES and n_fox % HEADS_PER_TILE == 0 and n_moba % HEADS_PER_TILE == 0

    assert MOBA_BLOCK + 1 >= MAX_DISTANCE
    rel_far = rel_bias[N_BUCKETS - 1] * LOG2E
    bias_tab = _bias_tables(rel_bias)

    cm, cf = w_moba // PROJ_CHUNK, w_fox // PROJ_CHUNK
    norm_chunks = (True,) * (2 * cm) + (False,) * cm + (True,) * (2 * cf) + (False,) * cf
    tiles_m, tiles_f = w_moba // LANES, w_fox // LANES
    ones_m, ones_f = jnp.ones((w_moba,), _F32), jnp.ones((w_fox,), _F32)

    for i in range(depth):
        gain_row = jnp.concatenate([
            jnp.tile(gq_moba[i] * (SCALE * LOG2E), n_moba), jnp.tile(gk_moba[i], n_moba), ones_m,
            jnp.tile(gq_fox[i] * (SCALE * LOG2E), n_fox), jnp.tile(gk_fox[i], n_fox), ones_f])[None, :]
        w_i = w_in[i].astype(_BF16)
        pad_f = -n_fox % 16
        wfT = jnp.pad(w_i[:, w_main:].T, ((0, pad_f), (0, 0)))
        b_col = jnp.pad(b_f[i], (0, pad_f))[:, None]
        qkv, cum = _proj_call(x, g_attn[i][None, :], w_i, wfT, b_col, gain_row,
                              norm_chunks, n_fox, tm=512)

        o_m = _moba_call(qkv, bias_tab, rel_far, tiles_m, 0, tiles_m, 2 * tiles_m)
        f0 = 3 * tiles_m
        o_f = _fox_call(qkv, cum, tiles_f, f0, f0 + tiles_f, f0 + 2 * tiles_f, tq=256)

        x = _mlp_call(
            x.reshape(B * S, D), o_m.reshape(B * S, w_moba), o_f.reshape(B * S, w_fox),
            p[i].reshape(B * S, -1), w_out[i].astype(_BF16), g_mlp[i][None, :],
            w_up[i].astype(_BF16), w_down[i].astype(_BF16), g_ple[i][None, :],
            w_ple_gate[i].astype(_BF16), w_ple_proj[i].astype(_BF16), tm=1024,
        ).reshape(B, S, D)
    return x
```

```python
import functools

import numpy as np
import jax
import jax.numpy as jnp
from jax import lax
from jax.experimental import pallas as pl
from jax.experimental.pallas import tpu as pltpu

HEAD_DIM = 64
MOBA_BLOCK = 256
MOBA_TOPK = 3
N_BUCKETS = 32
MAX_DISTANCE = 128
EPS = 1e-6
NEG = -1e30
SCALE = HEAD_DIM ** -0.5
LOG2E = 1.4426950408889634

LANES = 128
HEADS_PER_TILE = LANES // HEAD_DIM
PROJ_CHUNK = 512
QK_AHEAD = 2
ONES_ROWS = 16
MLP_SPLIT = 2
VMEM_LIMIT = 56 * 1024 * 1024

_F32 = jnp.float32
_BF16 = jnp.bfloat16


def _dot(a, b):
    return jnp.dot(a, b, preferred_element_type=_F32)


def _t5_bucket(rel):
    rel = jnp.maximum(rel, 0)
    max_exact = N_BUCKETS // 2
    relf = jnp.maximum(rel, max_exact).astype(_F32)
    large = max_exact + (jnp.log(relf / max_exact) / np.log(MAX_DISTANCE / max_exact)
                         * (N_BUCKETS - max_exact)).astype(jnp.int32)
    large = jnp.minimum(large, N_BUCKETS - 1)
    return jnp.where(rel < max_exact, rel, large)


def _bias_kernel(rbT_ref, bucket_ref, out_ref):
    n_heads, n_dist = rbT_ref.shape[0], bucket_ref.shape[1]
    blk = n_dist // 2
    bucket = bucket_ref[...]
    vals = jnp.zeros((n_heads, n_dist), _F32)
    for b in range(N_BUCKETS):
        vals = jnp.where(bucket == b, rbT_ref[:, b:b + 1], vals)
    vals = vals * -LOG2E
    key = lax.broadcasted_iota(jnp.int32, (blk, blk), 0)
    query = lax.broadcasted_iota(jnp.int32, (blk, blk), 1)
    for h in range(n_heads):
        skew = pltpu.roll(jnp.broadcast_to(vals[h:h + 1, :], (blk, n_dist)), 0, 1,
                          stride=1, stride_axis=0)
        out_ref[h, 0] = jnp.where(key <= query, skew[:, :blk], -NEG)
        out_ref[h, 1] = skew[:, blk:]


def _bias_tables(rel_bias):
    n_heads = rel_bias.shape[1]
    bucket = _t5_bucket(jnp.arange(2 * MOBA_BLOCK, dtype=jnp.int32))[None, :]
    blk = (MOBA_BLOCK, MOBA_BLOCK)
    return pl.pallas_call(
        _bias_kernel,
        out_shape=jax.ShapeDtypeStruct((n_heads, 2) + blk, _F32),
        name="moba_bias_tables",
    )(rel_bias.T, bucket)


def _proj_kernel(x_ref, g_ref, w_ref, wfT_ref, bf_ref, gain_ref, triu_ref,
                 qkv_ref, cum_ref, carry_ref, *, norm_chunks):
    tm = x_ref.shape[1]
    n_f = cum_ref.shape[1]

    @pl.when(pl.program_id(1) == 0)
    def _():
        carry_ref[...] = jnp.zeros_like(carry_ref)

    x = x_ref[0]
    ms = jnp.mean(x * x, axis=-1, keepdims=True)
    h = (x * lax.rsqrt(ms + EPS) * g_ref[...]).astype(_BF16)

    lane = lax.broadcasted_iota(jnp.int32, (1, LANES), 1)
    lo_mask = lane < HEAD_DIM
    for c, do_norm in enumerate(norm_chunks):
        acc = _dot(h, w_ref[:, c * PROJ_CHUNK:(c + 1) * PROJ_CHUNK])
        for s in range(PROJ_CHUNK // LANES):
            c0 = c * PROJ_CHUNK + s * LANES
            y = acc[:, s * LANES:(s + 1) * LANES]
            if do_norm:
                sq = y * y
                s_lo = jnp.sum(jnp.where(lo_mask, sq, 0.0), axis=-1, keepdims=True)
                s_hi = jnp.sum(jnp.where(lo_mask, 0.0, sq), axis=-1, keepdims=True)
                msq = jnp.where(lo_mask, s_lo, s_hi) * (1.0 / HEAD_DIM)
                y = y * lax.rsqrt(msq + EPS) * gain_ref[:, c0:c0 + LANES]
            qkv_ref[0, :, c0:c0 + LANES] = y.astype(_BF16)

    zT = bf_ref[...] + lax.dot_general(wfT_ref[...], h, (((1,), (1,)), ((), ())),
                                       preferred_element_type=_F32)
    lfT = (jnp.minimum(zT, 0.0) - jnp.log1p(jnp.exp(-jnp.abs(zT)))) * LOG2E
    hi = lfT.astype(_BF16)
    r1 = lfT - hi.astype(_F32)
    mid = r1.astype(_BF16)
    lo = (r1 - mid.astype(_F32)).astype(_BF16)
    triu = triu_ref[...]
    cumT = carry_ref[:, 0:1] + ((_dot(hi, triu) + _dot(mid, triu)) + _dot(lo, triu))
    carry_ref[...] = jnp.broadcast_to(cumT[:, tm - 1:tm], carry_ref.shape)
    cum_ref[0] = cumT[:n_f, :]


def _proj_call(x, g_attn, w_bf16, wfT, b_f_col, gain_row, norm_chunks, n_fox, tm):
    B, S, D = x.shape
    W = gain_row.shape[1]
    rows = wfT.shape[0]
    triu = jnp.triu(jnp.ones((tm, tm), _F32)).astype(_BF16)
    const = lambda shape: pl.BlockSpec(shape, lambda b, s: (0,) * len(shape))
    return pl.pallas_call(
        functools.partial(_proj_kernel, norm_chunks=norm_chunks),
        grid=(B, S // tm),
        in_specs=[pl.BlockSpec((1, tm, D), lambda b, s: (b, s, 0)),
                  const((1, D)), const(w_bf16.shape), const((rows, D)), const((rows, 1)),
                  const((1, W)), const((tm, tm))],
        out_specs=[pl.BlockSpec((1, tm, W), lambda b, s: (b, s, 0)),
                   pl.BlockSpec((1, n_fox, tm), lambda b, s: (b, 0, s))],
        out_shape=[jax.ShapeDtypeStruct((B, S, W), _BF16),
                   jax.ShapeDtypeStruct((B, n_fox, S), _F32)],
        scratch_shapes=[pltpu.VMEM((rows, LANES), _F32)],
        compiler_params=pltpu.CompilerParams(
            dimension_semantics=("parallel", "arbitrary"), vmem_limit_bytes=VMEM_LIMIT),
        name="proj_qknorm_cumsum",
    )(x, g_attn, w_bf16, wfT, b_f_col, gain_row, triu)


def _stage_heads(q_ref, v_ref, qT_scr, vT_scr):
    S = q_ref.shape[1]
    qT = q_ref[0].T
    vT = v_ref[0].T
    row = lax.broadcasted_iota(jnp.int32, (LANES, 1), 0)
    ones = jnp.ones((ONES_ROWS, S), _BF16)
    for h in range(HEADS_PER_TILE):
        mine = (row >= h * HEAD_DIM) & (row < (h + 1) * HEAD_DIM)
        qT_scr[h] = jnp.where(mine, qT, 0).astype(_BF16)
        vT_scr[h, :HEAD_DIM, :] = vT[h * HEAD_DIM:(h + 1) * HEAD_DIM, :]
        vT_scr[h, HEAD_DIM:, :] = ones


def _attend(nb, tq, k_ref, qT_scr, vT_scr, oT_scr, prep):
    heads = range(HEADS_PER_TILE)
    th = tq // 2
    pairs = [(qi, j) for qi in range(nb) for j in [qi] + list(range(qi))]
    qk = lambda qi, j: [[_dot(k_ref[0, j * tq + u * th:j * tq + (u + 1) * th, :],
                              qT_scr[h, :, qi * tq:(qi + 1) * tq]) for u in range(2)]
                        for h in heads]

    ahead = [qk(*pr) for pr in pairs[:QK_AHEAD]]
    acc = [None for _ in heads]
    m = [None for _ in heads]
    for idx, (qi, j) in enumerate(pairs):
        cur = ahead.pop(0)
        if idx + QK_AHEAD < len(pairs):
            ahead.append(qk(*pairs[idx + QK_AHEAD]))
        for h in heads:
            run = None if j == qi else m[h]
            ps, shifts = [], []
            for u in range(2):
                sT, r = prep(h, qi, j, u, cur[h][u])
                mx = jnp.max(sT, axis=0, keepdims=True)
                if r is not None:
                    mx = mx + r
                run = mx if run is None else jnp.maximum(run, mx)
                ps.append(jnp.exp2((sT - (run if r is None else run - r)).astype(_BF16)))
                shifts.append(run)
            ps[0] = ps[0] * jnp.exp2(shifts[0] - shifts[1]).astype(_BF16)
            pv = _dot(vT_scr[h, :, j * tq:(j + 1) * tq], jnp.concatenate(ps, axis=0))
            acc[h] = pv if j == qi else jnp.exp2(m[h] - run) * acc[h] + pv
            m[h] = run
            if j == max(qi - 1, 0):
                o = acc[h][:HEAD_DIM, :] * (1.0 / acc[h][HEAD_DIM:HEAD_DIM + 1, :])
                oT_scr[h * HEAD_DIM:(h + 1) * HEAD_DIM, qi * tq:(qi + 1) * tq] = o


def _attn_scratch(S):
    return [pltpu.VMEM((HEADS_PER_TILE, LANES, S), _BF16),
            pltpu.VMEM((HEADS_PER_TILE, HEAD_DIM + ONES_ROWS, S), _BF16),
            pltpu.VMEM((LANES, S), _F32)]


def _moba_kernel(far_ref, q_ref, k_ref, v_ref, nbias_ref, o_ref, qT_scr, vT_scr, oT_scr):
    S = q_ref.shape[1]
    tq = MOBA_BLOCK
    nb = S // tq
    hp = pl.program_id(1)
    _stage_heads(q_ref, v_ref, qT_scr, vT_scr)

    km_rows = [jnp.mean(k_ref[0, j * tq:(j + 1) * tq, :].astype(_F32), axis=0, keepdims=True)
               for j in range(nb)]
    km_rows += [jnp.zeros((1, LANES), _F32)] * (-nb % 16)
    km = jnp.concatenate(km_rows, axis=0)
    km_hi = km.astype(_BF16)
    km_lo = (km - km_hi.astype(_F32)).astype(_BF16)
    row = lax.broadcasted_iota(jnp.int32, (km.shape[0], tq), 0)
    far = [far_ref[hp * HEADS_PER_TILE + h] for h in range(HEADS_PER_TILE)]
    keep = {}

    def prep(h, qi, j, u, sT):
        th = tq // 2
        if j == qi:
            if qi > MOBA_TOPK and u == 0:
                w = qT_scr[h, :, qi * tq:(qi + 1) * tq]
                ge = jnp.where(row < qi, _dot(km_hi, w) + _dot(km_lo, w), -jnp.inf)
                for jj in range(qi):
                    gj = ge[jj:jj + 1, :]
                    ahead = (ge > gj) | ((ge == gj) & (row < jj))
                    rank = jnp.sum(jnp.where(ahead, 1.0, 0.0), axis=0, keepdims=True)
                    keep[h, qi, jj] = jnp.where(rank < MOBA_TOPK, 0.0, NEG)
            return sT - nbias_ref[h, 0, u * th:(u + 1) * th, :], None
        if j == qi - 1:
            return sT - nbias_ref[h, 1, u * th:(u + 1) * th, :], keep.get((h, qi, j))
        return sT, (keep[h, qi, j] + far[h] if (h, qi, j) in keep else far[h])

    _attend(nb, tq, k_ref, qT_scr, vT_scr, oT_scr, prep)
    o_ref[0] = oT_scr[...].T.astype(o_ref.dtype)


def _moba_call(qkv, bias_tab, rel_far, n_tiles, q_off, k_off, v_off):
    B, S, _ = qkv.shape
    blk = (1, S, LANES)
    return pl.pallas_call(
        _moba_kernel,
        grid=(B, n_tiles),
        in_specs=[pl.BlockSpec(memory_space=pltpu.SMEM),
                  pl.BlockSpec(blk, lambda b, t: (b, 0, q_off + t)),
                  pl.BlockSpec(blk, lambda b, t: (b, 0, k_off + t)),
                  pl.BlockSpec(blk, lambda b, t: (b, 0, v_off + t)),
                  pl.BlockSpec((HEADS_PER_TILE, 2, MOBA_BLOCK, MOBA_BLOCK),
                               lambda b, t: (t, 0, 0, 0))],
        out_specs=pl.BlockSpec(blk, lambda b, t: (b, 0, t)),
        out_shape=jax.ShapeDtypeStruct((B, S, n_tiles * LANES), _BF16),
        scratch_shapes=_attn_scratch(S),
        compiler_params=pltpu.CompilerParams(
            dimension_semantics=("parallel", "arbitrary"), vmem_limit_bytes=VMEM_LIMIT),
        name="moba_attention",
    )(rel_far, qkv, qkv, qkv, bias_tab)


def _fox_kernel(q_ref, k_ref, v_ref, cum_ref, causal_ref, o_ref,
                qT_scr, vT_scr, oT_scr, ckb_scr):
    S = q_ref.shape[1]
    tq = causal_ref.shape[0]
    nb = S // tq
    head0 = pl.program_id(1) * HEADS_PER_TILE
    _stage_heads(q_ref, v_ref, qT_scr, vT_scr)
    for h in range(HEADS_PER_TILE):
        ckb_scr[h] = jnp.broadcast_to(cum_ref[0, pl.ds(head0 + h, 1), :], (LANES, S)).T

    def prep(h, qi, j, u, sT):
        th = tq // 2
        ckb = ckb_scr[h, j * tq + u * th:j * tq + (u + 1) * th, :]
        sT = sT - jnp.concatenate([ckb, ckb], axis=1)
        if j == qi:
            sT = sT + causal_ref[u * th:(u + 1) * th, :]
        return sT, cum_ref[0, pl.ds(head0 + h, 1), qi * tq:(qi + 1) * tq]

    _attend(nb, tq, k_ref, qT_scr, vT_scr, oT_scr, prep)
    o_ref[0] = oT_scr[...].T.astype(o_ref.dtype)


def _fox_call(qkv, cum, n_tiles, q_off, k_off, v_off, tq):
    B, S, _ = qkv.shape
    blk = (1, S, LANES)
    t = jnp.arange(tq, dtype=jnp.int32)
    causal = jnp.where(t[:, None] <= t[None, :], 0.0, NEG).astype(_F32)
    return pl.pallas_call(
        _fox_kernel,
        grid=(B, n_tiles),
        in_specs=[pl.BlockSpec(blk, lambda b, t: (b, 0, q_off + t)),
                  pl.BlockSpec(blk, lambda b, t: (b, 0, k_off + t)),
                  pl.BlockSpec(blk, lambda b, t: (b, 0, v_off + t)),
                  pl.BlockSpec((1,) + cum.shape[1:], lambda b, t: (b, 0, 0)),
                  pl.BlockSpec((tq, tq), lambda b, t: (0, 0))],
        out_specs=pl.BlockSpec(blk, lambda b, t: (b, 0, t)),
        out_shape=jax.ShapeDtypeStruct((B, S, n_tiles * LANES), _BF16),
        scratch_shapes=_attn_scratch(S) + [pltpu.VMEM((HEADS_PER_TILE, S, LANES), _F32)],
        compiler_params=pltpu.CompilerParams(
            dimension_semantics=("parallel", "arbitrary"), vmem_limit_bytes=VMEM_LIMIT),
        name="fox_attention",
    )(qkv, qkv, qkv, cum, causal)


def _rms(x, g):
    ms = jnp.mean(x * x, axis=-1, keepdims=True)
    return x * lax.rsqrt(ms + EPS) * g


def _mlp_kernel(x_ref, om_ref, of_ref, p_ref, wo_ref, gm_ref, wu_ref, wd_ref,
                gp_ref, wg_ref, wp_ref, out_ref, *, ff_chunk):
    wm = om_ref.shape[1]
    tm = x_ref.shape[0]
    groups = [slice(i * (tm // MLP_SPLIT), (i + 1) * (tm // MLP_SPLIT)) for i in range(MLP_SPLIT)]
    x1 = [x_ref[r, :] + (_dot(om_ref[r, :], wo_ref[:wm, :]) + _dot(of_ref[r, :], wo_ref[wm:, :]))
          for r in groups]
    h = [_rms(v, gm_ref[...]).astype(_BF16) for v in x1]
    y = [None] * MLP_SPLIT
    for c in range(wu_ref.shape[1] // ff_chunk):
        for i in range(MLP_SPLIT):
            u = _dot(h[i], wu_ref[:, c * ff_chunk:(c + 1) * ff_chunk])
            a = jnp.square(jnp.maximum(u, 0.0)).astype(_BF16)
            d = _dot(a, wd_ref[c * ff_chunk:(c + 1) * ff_chunk, :])
            y[i] = d if y[i] is None else y[i] + d
    for i, r in enumerate(groups):
        x2 = x1[i] + y[i]
        hg = _rms(x2, gp_ref[...]).astype(_BF16)
        gate = jax.nn.sigmoid(_dot(hg, wg_ref[...]))
        out_ref[r, :] = x2 + gate * _dot(p_ref[r, :].astype(_BF16), wp_ref[...])


def _mlp_call(x2d, o_m, o_f, p2d, w_out, g_mlp, w_up, w_down, g_ple, w_gate, w_proj, tm):
    T, D = x2d.shape
    row = lambda w: pl.BlockSpec((tm, w), lambda i: (i, 0))
    const = lambda a: pl.BlockSpec(a.shape, lambda i: (0, 0), pipeline_mode=pl.Buffered(1))
    return pl.pallas_call(
        functools.partial(_mlp_kernel, ff_chunk=512),
        grid=(T // tm,),
        in_specs=[row(D), row(o_m.shape[1]), row(o_f.shape[1]), row(p2d.shape[1]),
                  const(w_out), const(g_mlp), const(w_up), const(w_down),
                  const(g_ple), const(w_gate), const(w_proj)],
        out_specs=row(D),
        out_shape=jax.ShapeDtypeStruct((T, D), _F32),
        compiler_params=pltpu.CompilerParams(
            dimension_semantics=("parallel",), vmem_limit_bytes=VMEM_LIMIT),
        name="outproj_mlp_ple",
    )(x2d, o_m, o_f, p2d, w_out, g_mlp, w_up, w_down, g_ple, w_gate, w_proj)


def kernel(x, p, rel_bias, g_attn, w_in, b_f, gq_moba, gk_moba, gq_fox, gk_fox,
           w_out, g_mlp, w_up, w_down, g_ple, w_ple_gate, w_ple_proj):
    B, S, D = x.shape
    depth = p.shape[0]
    n_fox = b_f.shape[1]
    n_moba = rel_bias.shape[1]
    w_moba, w_fox = n_moba * HEAD_DIM, n_fox * HEAD_DIM
    w_main = 3 * w_moba + 3 * w_fox
    assert w_in.shape[2] == w_main + n_fox
    assert S % MOBA_BLOCK == 0 and w_moba % PROJ_CHUNK == 0 and w_fox % PROJ_CHUNK == 0
    assert n_fox <= LANES and n_fox % HEADS_PER_TILE == 0 and n_moba % HEADS_PER_TILE == 0

    assert MOBA_BLOCK + 1 >= MAX_DISTANCE
    rel_far = rel_bias[N_BUCKETS - 1] * LOG2E
    bias_tab = _bias_tables(rel_bias)

    cm, cf = w_moba // PROJ_CHUNK, w_fox // PROJ_CHUNK
    norm_chunks = (True,) * (2 * cm) + (False,) * cm + (True,) * (2 * cf) + (False,) * cf
    tiles_m, tiles_f = w_moba // LANES, w_fox // LANES
    ones_m, ones_f = jnp.ones((w_moba,), _F32), jnp.ones((w_fox,), _F32)

    for i in range(depth):
        gain_row = jnp.concatenate([
            jnp.tile(gq_moba[i] * (SCALE * LOG2E), n_moba), jnp.tile(gk_moba[i], n_moba), ones_m,
            jnp.tile(gq_fox[i] * (SCALE * LOG2E), n_fox), jnp.tile(gk_fox[i], n_fox), ones_f])[None, :]
        w_i = w_in[i].astype(_BF16)
        pad_f = -n_fox % 16
        wfT = jnp.pad(w_i[:, w_main:].T, ((0, pad_f), (0, 0)))
        b_col = jnp.pad(b_f[i], (0, pad_f))[:, None]
        qkv, cum = _proj_call(x, g_attn[i][None, :], w_i, wfT, b_col, gain_row,
                              norm_chunks, n_fox, tm=512)

        o_m = _moba_call(qkv, bias_tab, rel_far, tiles_m, 0, tiles_m, 2 * tiles_m)
        f0 = 3 * tiles_m
        o_f = _fox_call(qkv, cum, tiles_f, f0, f0 + tiles_f, f0 + 2 * tiles_f, tq=256)

        x = _mlp_call(
            x.reshape(B * S, D), o_m.reshape(B * S, w_moba), o_f.reshape(B * S, w_fox),
            p[i].reshape(B * S, -1), w_out[i].astype(_BF16), g_mlp[i][None, :],
            w_up[i].astype(_BF16), w_down[i].astype(_BF16), g_ple[i][None, :],
            w_ple_gate[i].astype(_BF16), w_ple_proj[i].astype(_BF16), tm=1024,
        ).reshape(B, S, D)
    return x
```

```python
import functools

import numpy as np
import jax
import jax.numpy as jnp
from jax import lax
from jax.experimental import pallas as pl
from jax.experimental.pallas import tpu as pltpu

HEAD_DIM = 64
MOBA_BLOCK = 256
MOBA_TOPK = 3
N_BUCKETS = 32
MAX_DISTANCE = 128
EPS = 1e-6
NEG = -1e30
SCALE = HEAD_DIM ** -0.5
LOG2E = 1.4426950408889634

LANES = 128
HEADS_PER_TILE = LANES // HEAD_DIM
PROJ_CHUNK = 512
QK_AHEAD = 2
ONES_ROWS = 16
ATTN_TILES = 2
MLP_SPLIT = 2
VMEM_LIMIT = 56 * 1024 * 1024

_F32 = jnp.float32
_BF16 = jnp.bfloat16


def _dot(a, b):
    return jnp.dot(a, b, preferred_element_type=_F32)


def _t5_bucket(rel):
    rel = np.maximum(np.asarray(rel), 0)
    max_exact = N_BUCKETS // 2
    relf = np.maximum(rel, max_exact).astype(np.float32)
    large = max_exact + (np.log(relf / max_exact) / np.log(MAX_DISTANCE / max_exact)
                         * (N_BUCKETS - max_exact)).astype(np.int32)
    large = np.minimum(large, N_BUCKETS - 1)
    return np.where(rel < max_exact, rel, large).astype(np.int32)


def _bias_kernel(rbT_ref, bucket_ref, out_ref):
    n_heads, n_dist = rbT_ref.shape[0], bucket_ref.shape[1]
    blk = n_dist // 2
    bucket = bucket_ref[...]
    vals = jnp.zeros((n_heads, n_dist), _F32)
    for b in range(N_BUCKETS):
        vals = jnp.where(bucket == b, rbT_ref[:, b:b + 1], vals)
    vals = vals * -LOG2E
    key = lax.broadcasted_iota(jnp.int32, (blk, blk), 0)
    query = lax.broadcasted_iota(jnp.int32, (blk, blk), 1)
    for h in range(n_heads):
        skew = pltpu.roll(jnp.broadcast_to(vals[h:h + 1, :], (blk, n_dist)), 0, 1,
                          stride=1, stride_axis=0)
        out_ref[h, 0] = jnp.where(key <= query, skew[:, :blk], -NEG)
        out_ref[h, 1] = skew[:, blk:]


def _bias_tables(rel_bias):
    n_heads = rel_bias.shape[1]
    bucket = jnp.asarray(_t5_bucket(np.arange(2 * MOBA_BLOCK))[None, :])
    blk = (MOBA_BLOCK, MOBA_BLOCK)
    return pl.pallas_call(
        _bias_kernel,
        out_shape=jax.ShapeDtypeStruct((n_heads, 2) + blk, _F32),
        name="moba_bias_tables",
    )(rel_bias.T, bucket)


def _proj_kernel(x_ref, g_ref, w_ref, wfT_ref, bf_ref, gain_ref, triu_ref,
                 qkv_ref, cum_ref, carry_ref, *, norm_chunks):
    tm = x_ref.shape[1]
    n_f = cum_ref.shape[1]

    @pl.when(pl.program_id(1) == 0)
    def _():
        carry_ref[...] = jnp.zeros_like(carry_ref)

    x = x_ref[0]
    ms = jnp.mean(x * x, axis=-1, keepdims=True)
    h = (x * lax.rsqrt(ms + EPS) * g_ref[...]).astype(_BF16)

    lane = lax.broadcasted_iota(jnp.int32, (1, LANES), 1)
    lo_mask = lane < HEAD_DIM
    for c, do_norm in enumerate(norm_chunks):
        acc = _dot(h, w_ref[:, c * PROJ_CHUNK:(c + 1) * PROJ_CHUNK])
        for s in range(PROJ_CHUNK // LANES):
            c0 = c * PROJ_CHUNK + s * LANES
            y = acc[:, s * LANES:(s + 1) * LANES]
            if do_norm:
                sq = y * y
                s_lo = jnp.sum(jnp.where(lo_mask, sq, 0.0), axis=-1, keepdims=True)
                s_hi = jnp.sum(jnp.where(lo_mask, 0.0, sq), axis=-1, keepdims=True)
                msq = jnp.where(lo_mask, s_lo, s_hi) * (1.0 / HEAD_DIM)
                y = y * lax.rsqrt(msq + EPS) * gain_ref[:, c0:c0 + LANES]
            qkv_ref[0, :, c0:c0 + LANES] = y.astype(_BF16)

    zT = bf_ref[...] + lax.dot_general(wfT_ref[...], h, (((1,), (1,)), ((), ())),
                                       preferred_element_type=_F32)
    lfT = (jnp.minimum(zT, 0.0) - jnp.log1p(jnp.exp(-jnp.abs(zT)))) * LOG2E
    hi = lfT.astype(_BF16)
    r1 = lfT - hi.astype(_F32)
    mid = r1.astype(_BF16)
    lo = (r1 - mid.astype(_F32)).astype(_BF16)
    triu = triu_ref[...]
    cumT = carry_ref[:, 0:1] + ((_dot(hi, triu) + _dot(mid, triu)) + _dot(lo, triu))
    carry_ref[...] = jnp.broadcast_to(cumT[:, tm - 1:tm], carry_ref.shape)
    cum_ref[0] = cumT[:n_f, :]


def _proj_call(x, g_attn, w_bf16, wfT, b_f_col, gain_row, norm_chunks, n_fox, tm):
    B, S, D = x.shape
    W = gain_row.shape[1]
    rows = wfT.shape[0]
    triu = jnp.triu(jnp.ones((tm, tm), _F32)).astype(_BF16)
    const = lambda shape: pl.BlockSpec(shape, lambda b, s: (0,) * len(shape))
    return pl.pallas_call(
        functools.partial(_proj_kernel, norm_chunks=norm_chunks),
        grid=(B, S // tm),
        in_specs=[pl.BlockSpec((1, tm, D), lambda b, s: (b, s, 0)),
                  const((1, D)), const(w_bf16.shape), const((rows, D)), const((rows, 1)),
                  const((1, W)), const((tm, tm))],
        out_specs=[pl.BlockSpec((1, tm, W), lambda b, s: (b, s, 0)),
                   pl.BlockSpec((1, n_fox, tm), lambda b, s: (b, 0, s))],
        out_shape=[jax.ShapeDtypeStruct((B, S, W), _BF16),
                   jax.ShapeDtypeStruct((B, n_fox, S), _F32)],
        scratch_shapes=[pltpu.VMEM((rows, LANES), _F32)],
        compiler_params=pltpu.CompilerParams(
            dimension_semantics=("parallel", "arbitrary"), vmem_limit_bytes=VMEM_LIMIT),
        name="proj_qknorm_cumsum",
    )(x, g_attn, w_bf16, wfT, b_f_col, gain_row, triu)


def _stage_heads(q_ref, v_ref, qT_scr, vT_scr, t):
    S = q_ref.shape[1]
    row = lax.broadcasted_iota(jnp.int32, (LANES, 1), 0)
    ones = jnp.ones((ONES_ROWS, S), _BF16)
    qT = q_ref[0, :, t * LANES:(t + 1) * LANES].T
    vT = v_ref[0, :, t * LANES:(t + 1) * LANES].T
    for hl in range(HEADS_PER_TILE):
        h = t * HEADS_PER_TILE + hl
        mine = (row >= hl * HEAD_DIM) & (row < (hl + 1) * HEAD_DIM)
        qT_scr[h] = jnp.where(mine, qT, 0).astype(_BF16)
        vT_scr[h, :HEAD_DIM, :] = vT[hl * HEAD_DIM:(hl + 1) * HEAD_DIM, :]
        vT_scr[h, HEAD_DIM:, :] = ones


def _attend(nb, tq, k_ref, qT_scr, vT_scr, oT_scr, prep, stage):
    tiles = qT_scr.shape[0] // HEADS_PER_TILE
    th = tq // 2
    pairs = [(t, qi, j) for t in range(tiles) for qi in range(nb)
             for j in [qi] + list(range(qi))]
    heads_of = lambda t: range(t * HEADS_PER_TILE, (t + 1) * HEADS_PER_TILE)

    def qk(t, qi, j):
        if qi == 0:
            stage(t)
        lanes = slice(t * LANES, (t + 1) * LANES)
        return {h: [_dot(k_ref[0, j * tq + u * th:j * tq + (u + 1) * th, lanes],
                         qT_scr[h, :, qi * tq:(qi + 1) * tq]) for u in range(2)]
                for h in heads_of(t)}

    ahead = [qk(*pr) for pr in pairs[:QK_AHEAD]]
    acc, m = {}, {}
    for idx, (t, qi, j) in enumerate(pairs):
        cur = ahead.pop(0)
        if idx + QK_AHEAD < len(pairs):
            ahead.append(qk(*pairs[idx + QK_AHEAD]))
        for h in heads_of(t):
            run = None if j == qi else m[h]
            ps, shifts = [], []
            for u in range(2):
                sT, r = prep(h, qi, j, u, cur[h][u])
                mx = jnp.max(sT, axis=0, keepdims=True)
                if r is not None:
                    mx = mx + r
                run = mx if run is None else jnp.maximum(run, mx)
                ps.append(jnp.exp2((sT - (run if r is None else run - r)).astype(_BF16)))
                shifts.append(run)
            ps[0] = ps[0] * jnp.exp2(shifts[0] - shifts[1]).astype(_BF16)
            pv = _dot(vT_scr[h, :, j * tq:(j + 1) * tq], jnp.concatenate(ps, axis=0))
            acc[h] = pv if j == qi else jnp.exp2(m[h] - run) * acc[h] + pv
            m[h] = run
            if j == max(qi - 1, 0):
                o = acc[h][:HEAD_DIM, :] * (1.0 / acc[h][HEAD_DIM:HEAD_DIM + 1, :])
                oT_scr[h * HEAD_DIM:(h + 1) * HEAD_DIM, qi * tq:(qi + 1) * tq] = o


def _attn_scratch(S):
    nh = ATTN_TILES * HEADS_PER_TILE
    return [pltpu.VMEM((nh, LANES, S), _BF16),
            pltpu.VMEM((nh, HEAD_DIM + ONES_ROWS, S), _BF16),
            pltpu.VMEM((ATTN_TILES * LANES, S), _F32)]


def _moba_kernel(far_ref, q_ref, k_ref, v_ref, nbias_ref, o_ref, qT_scr, vT_scr, oT_scr):
    S = q_ref.shape[1]
    tq = MOBA_BLOCK
    nb = S // tq
    nh = qT_scr.shape[0]
    head0 = pl.program_id(1) * nh
    stage = lambda t: _stage_heads(q_ref, v_ref, qT_scr, vT_scr, t)

    km_hi, km_lo = [], []
    for t in range(nh // HEADS_PER_TILE):
        rows = [jnp.mean(k_ref[0, j * tq:(j + 1) * tq, t * LANES:(t + 1) * LANES].astype(_F32),
                         axis=0, keepdims=True) for j in range(nb)]
        rows += [jnp.zeros((1, LANES), _F32)] * (-nb % 16)
        km = jnp.concatenate(rows, axis=0)
        km_hi.append(km.astype(_BF16))
        km_lo.append((km - km_hi[-1].astype(_F32)).astype(_BF16))
    row = lax.broadcasted_iota(jnp.int32, (km_hi[0].shape[0], tq), 0)
    far = [far_ref[head0 + h] for h in range(nh)]
    keep = {}

    def prep(h, qi, j, u, sT):
        th = tq // 2
        if j == qi:
            if qi > MOBA_TOPK and u == 0:
                w = qT_scr[h, :, qi * tq:(qi + 1) * tq]
                t = h // HEADS_PER_TILE
                ge = jnp.where(row < qi, _dot(km_hi[t], w) + _dot(km_lo[t], w), -jnp.inf)
                for jj in range(qi):
                    gj = ge[jj:jj + 1, :]
                    ahead = (ge > gj) | ((ge == gj) & (row < jj))
                    rank = jnp.sum(jnp.where(ahead, 1.0, 0.0), axis=0, keepdims=True)
                    keep[h, qi, jj] = jnp.where(rank < MOBA_TOPK, 0.0, NEG)
            return sT - nbias_ref[h, 0, u * th:(u + 1) * th, :], None
        if j == qi - 1:
            return sT - nbias_ref[h, 1, u * th:(u + 1) * th, :], keep.get((h, qi, j))
        return sT, (keep[h, qi, j] + far[h] if (h, qi, j) in keep else far[h])

    _attend(nb, tq, k_ref, qT_scr, vT_scr, oT_scr, prep, stage)
    o_ref[0] = oT_scr[...].T.astype(o_ref.dtype)


def _moba_call(qkv, bias_tab, rel_far, n_tiles, q_off, k_off, v_off):
    B, S, _ = qkv.shape
    assert all(v % ATTN_TILES == 0 for v in (n_tiles, q_off, k_off, v_off))
    blk = (1, S, ATTN_TILES * LANES)
    nh = ATTN_TILES * HEADS_PER_TILE
    return pl.pallas_call(
        _moba_kernel,
        grid=(B, n_tiles // ATTN_TILES),
        in_specs=[pl.BlockSpec(memory_space=pltpu.SMEM),
                  pl.BlockSpec(blk, lambda b, t: (b, 0, q_off // ATTN_TILES + t)),
                  pl.BlockSpec(blk, lambda b, t: (b, 0, k_off // ATTN_TILES + t)),
                  pl.BlockSpec(blk, lambda b, t: (b, 0, v_off // ATTN_TILES + t)),
                  pl.BlockSpec((nh, 2, MOBA_BLOCK, MOBA_BLOCK), lambda b, t: (t, 0, 0, 0))],
        out_specs=pl.BlockSpec(blk, lambda b, t: (b, 0, t)),
        out_shape=jax.ShapeDtypeStruct((B, S, n_tiles * LANES), _BF16),
        scratch_shapes=_attn_scratch(S),
        compiler_params=pltpu.CompilerParams(
            dimension_semantics=("parallel", "arbitrary"), vmem_limit_bytes=VMEM_LIMIT),
        name="moba_attention",
    )(rel_far, qkv, qkv, qkv, bias_tab)


def _fox_kernel(q_ref, k_ref, v_ref, cum_ref, causal_ref, o_ref,
                qT_scr, vT_scr, oT_scr, ckb_scr):
    S = q_ref.shape[1]
    tq = causal_ref.shape[0]
    nb = S // tq
    head0 = pl.program_id(1) * qT_scr.shape[0]

    def stage(t):
        _stage_heads(q_ref, v_ref, qT_scr, vT_scr, t)
        for h in range(t * HEADS_PER_TILE, (t + 1) * HEADS_PER_TILE):
            ckb_scr[h] = jnp.broadcast_to(cum_ref[0, pl.ds(head0 + h, 1), :], (LANES, S)).T

    def prep(h, qi, j, u, sT):
        th = tq // 2
        ckb = ckb_scr[h, j * tq + u * th:j * tq + (u + 1) * th, :]
        sT = sT - jnp.concatenate([ckb, ckb], axis=1)
        if j == qi:
            sT = sT + causal_ref[u * th:(u + 1) * th, :]
        return sT, cum_ref[0, pl.ds(head0 + h, 1), qi * tq:(qi + 1) * tq]

    _attend(nb, tq, k_ref, qT_scr, vT_scr, oT_scr, prep, stage)
    o_ref[0] = oT_scr[...].T.astype(o_ref.dtype)


def _fox_call(qkv, cum, n_tiles, q_off, k_off, v_off, tq):
    B, S, _ = qkv.shape
    assert all(v % ATTN_TILES == 0 for v in (n_tiles, q_off, k_off, v_off))
    blk = (1, S, ATTN_TILES * LANES)
    t = jnp.arange(tq, dtype=jnp.int32)
    causal = jnp.where(t[:, None] <= t[None, :], 0.0, NEG).astype(_F32)
    return pl.pallas_call(
        _fox_kernel,
        grid=(B, n_tiles // ATTN_TILES),
        in_specs=[pl.BlockSpec(blk, lambda b, t: (b, 0, q_off // ATTN_TILES + t)),
                  pl.BlockSpec(blk, lambda b, t: (b, 0, k_off // ATTN_TILES + t)),
                  pl.BlockSpec(blk, lambda b, t: (b, 0, v_off // ATTN_TILES + t)),
                  pl.BlockSpec((1,) + cum.shape[1:], lambda b, t: (b, 0, 0)),
                  pl.BlockSpec((tq, tq), lambda b, t: (0, 0))],
        out_specs=pl.BlockSpec(blk, lambda b, t: (b, 0, t)),
        out_shape=jax.ShapeDtypeStruct((B, S, n_tiles * LANES), _BF16),
        scratch_shapes=_attn_scratch(S) + [pltpu.VMEM((ATTN_TILES * HEADS_PER_TILE, S, LANES), _F32)],
        compiler_params=pltpu.CompilerParams(
            dimension_semantics=("parallel", "arbitrary"), vmem_limit_bytes=VMEM_LIMIT),
        name="fox_attention",
    )(qkv, qkv, qkv, cum, causal)


def _rms(x, g):
    ms = jnp.mean(x * x, axis=-1, keepdims=True)
    return x * lax.rsqrt(ms + EPS) * g


def _mlp_kernel(x_ref, om_ref, of_ref, p_ref, wo_ref, gm_ref, wu_ref, wd_ref,
                gp_ref, wg_ref, wp_ref, out_ref, *, ff_chunk):
    wm = om_ref.shape[1]
    tm = x_ref.shape[0]
    groups = [slice(i * (tm // MLP_SPLIT), (i + 1) * (tm // MLP_SPLIT)) for i in range(MLP_SPLIT)]
    x1 = [x_ref[r, :] + (_dot(om_ref[r, :], wo_ref[:wm, :]) + _dot(of_ref[r, :], wo_ref[wm:, :]))
          for r in groups]
    h = [_rms(v, gm_ref[...]).astype(_BF16) for v in x1]
    y = [None] * MLP_SPLIT
    for c in range(wu_ref.shape[1] // ff_chunk):
        for i in range(MLP_SPLIT):
            u = _dot(h[i], wu_ref[:, c * ff_chunk:(c + 1) * ff_chunk])
            a = jnp.square(jnp.maximum(u, 0.0)).astype(_BF16)
            d = _dot(a, wd_ref[c * ff_chunk:(c + 1) * ff_chunk, :])
            y[i] = d if y[i] is None else y[i] + d
    for i, r in enumerate(groups):
        x2 = x1[i] + y[i]
        hg = _rms(x2, gp_ref[...]).astype(_BF16)
        gate = jax.nn.sigmoid(_dot(hg, wg_ref[...]))
        out_ref[r, :] = x2 + gate * _dot(p_ref[r, :].astype(_BF16), wp_ref[...])


def _mlp_call(x2d, o_m, o_f, p2d, w_out, g_mlp, w_up, w_down, g_ple, w_gate, w_proj, tm):
    T, D = x2d.shape
    row = lambda w: pl.BlockSpec((tm, w), lambda i: (i, 0))
    const = lambda a: pl.BlockSpec(a.shape, lambda i: (0, 0), pipeline_mode=pl.Buffered(1))
    return pl.pallas_call(
        functools.partial(_mlp_kernel, ff_chunk=512),
        grid=(T // tm,),
        in_specs=[row(D), row(o_m.shape[1]), row(o_f.shape[1]), row(p2d.shape[1]),
                  const(w_out), const(g_mlp), const(w_up), const(w_down),
                  const(g_ple), const(w_gate), const(w_proj)],
        out_specs=row(D),
        out_shape=jax.ShapeDtypeStruct((T, D), _F32),
        compiler_params=pltpu.CompilerParams(
            dimension_semantics=("parallel",), vmem_limit_bytes=VMEM_LIMIT),
        name="outproj_mlp_ple",
    )(x2d, o_m, o_f, p2d, w_out, g_mlp, w_up, w_down, g_ple, w_gate, w_proj)


def kernel(x, p, rel_bias, g_attn, w_in, b_f, gq_moba, gk_moba, gq_fox, gk_fox,
           w_out, g_mlp, w_up, w_down, g_ple, w_ple_gate, w_ple_proj):
    B, S, D = x.shape
    depth = p.shape[0]
    n_fox = b_f.shape[1]
    n_moba = rel_bias.shape[1]
    w_moba, w_fox = n_moba * HEAD_DIM, n_fox * HEAD_DIM
    w_main = 3 * w_moba + 3 * w_fox
    assert w_in.shape[2] == w_main + n_fox
    assert S % MOBA_BLOCK == 0 and w_moba % PROJ_CHUNK == 0 and w_fox % PROJ_CHUNK == 0
    assert n_fox <= LANES and n_fox % HEADS_PER_TILE == 0 and n_moba % HEADS_PER_TILE == 0

    assert MOBA_BLOCK + 1 >= MAX_DISTANCE
    rel_far = rel_bias[N_BUCKETS - 1] * LOG2E
    bias_tab = _bias_tables(rel_bias)

    cm, cf = w_moba // PROJ_CHUNK, w_fox // PROJ_CHUNK
    norm_chunks = (True,) * (2 * cm) + (False,) * cm + (True,) * (2 * cf) + (False,) * cf
    tiles_m, tiles_f = w_moba // LANES, w_fox // LANES
    ones_m, ones_f = jnp.ones((w_moba,), _F32), jnp.ones((w_fox,), _F32)

    for i in range(depth):
        gain_row = jnp.concatenate([
            jnp.tile(gq_moba[i] * (SCALE * LOG2E), n_moba), jnp.tile(gk_moba[i], n_moba), ones_m,
            jnp.tile(gq_fox[i] * (SCALE * LOG2E), n_fox), jnp.tile(gk_fox[i], n_fox), ones_f])[None, :]
        w_i = w_in[i].astype(_BF16)
        pad_f = -n_fox % 16
        wfT = jnp.pad(w_i[:, w_main:].T, ((0, pad_f), (0, 0)))
        b_col = jnp.pad(b_f[i], (0, pad_f))[:, None]
        qkv, cum = _proj_call(x, g_attn[i][None, :], w_i, wfT, b_col, gain_row,
                              norm_chunks, n_fox, tm=512)

        o_m = _moba_call(qkv, bias_tab, rel_far, tiles_m, 0, tiles_m, 2 * tiles_m)
        f0 = 3 * tiles_m
        o_f = _fox_call(qkv, cum, tiles_f, f0, f0 + tiles_f, f0 + 2 * tiles_f, tq=256)

        x = _mlp_call(
            x.reshape(B * S, D), o_m.reshape(B * S, w_moba), o_f.reshape(B * S, w_fox),
            p[i].reshape(B * S, -1), w_out[i].astype(_BF16), g_mlp[i][None, :],
            w_up[i].astype(_BF16), w_down[i].astype(_BF16), g_ple[i][None, :],
            w_ple_gate[i].astype(_BF16), w_ple_proj[i].astype(_BF16), tm=1024,
        ).reshape(B, S, D)
    return x
```

```python
import functools

import numpy as np
import jax
import jax.numpy as jnp
from jax import lax
from jax.experimental import pallas as pl
from jax.experimental.pallas import tpu as pltpu

HEAD_DIM = 64
MOBA_BLOCK = 256
MOBA_TOPK = 3
N_BUCKETS = 32
MAX_DISTANCE = 128
EPS = 1e-6
NEG = -1e30
SCALE = HEAD_DIM ** -0.5
LOG2E = 1.4426950408889634

LANES = 128
HEADS_PER_TILE = LANES // HEAD_DIM
PROJ_CHUNK = 512
PROJ_GROUP = 512
QK_AHEAD = 2
ONES_ROWS = 16
ATTN_TILES = 2
MLP_SPLIT = 2
VMEM_LIMIT = 56 * 1024 * 1024

_F32 = jnp.float32
_BF16 = jnp.bfloat16


def _dot(a, b):
    return jnp.dot(a, b, preferred_element_type=_F32)


def _t5_bucket(rel):
    rel = np.maximum(np.asarray(rel), 0)
    max_exact = N_BUCKETS // 2
    relf = np.maximum(rel, max_exact).astype(np.float32)
    large = max_exact + (np.log(relf / max_exact) / np.log(MAX_DISTANCE / max_exact)
                         * (N_BUCKETS - max_exact)).astype(np.int32)
    large = np.minimum(large, N_BUCKETS - 1)
    return np.where(rel < max_exact, rel, large).astype(np.int32)


def _bias_kernel(rbT_ref, bucket_ref, out_ref):
    n_heads, n_dist = rbT_ref.shape[0], bucket_ref.shape[1]
    blk = n_dist // 2
    bucket = bucket_ref[...]
    vals = jnp.zeros((n_heads, n_dist), _F32)
    for b in range(N_BUCKETS):
        vals = jnp.where(bucket == b, rbT_ref[:, b:b + 1], vals)
    vals = vals * -LOG2E
    key = lax.broadcasted_iota(jnp.int32, (blk, blk), 0)
    query = lax.broadcasted_iota(jnp.int32, (blk, blk), 1)
    for h in range(n_heads):
        skew = pltpu.roll(jnp.broadcast_to(vals[h:h + 1, :], (blk, n_dist)), 0, 1,
                          stride=1, stride_axis=0)
        out_ref[h, 0] = jnp.where(key <= query, skew[:, :blk], -NEG)
        out_ref[h, 1] = skew[:, blk:]


def _bias_tables(rel_bias):
    n_heads = rel_bias.shape[1]
    bucket = jnp.asarray(_t5_bucket(np.arange(2 * MOBA_BLOCK))[None, :])
    blk = (MOBA_BLOCK, MOBA_BLOCK)
    return pl.pallas_call(
        _bias_kernel,
        out_shape=jax.ShapeDtypeStruct((n_heads, 2) + blk, _F32),
        name="moba_bias_tables",
    )(rel_bias.T, bucket)


def _proj_kernel(x_ref, g_ref, w_ref, wfT_ref, bf_ref, gain_ref, triu_ref,
                 qkv_ref, cum_ref, carry_ref, *, norm_chunks):
    tm = x_ref.shape[1]
    n_f = cum_ref.shape[1]

    @pl.when(pl.program_id(1) == 0)
    def _():
        carry_ref[...] = jnp.zeros_like(carry_ref)

    lane = lax.broadcasted_iota(jnp.int32, (1, LANES), 1)
    lo_mask = lane < HEAD_DIM
    triu = triu_ref[...]
    tg = triu.shape[0]
    carry = carry_ref[:, 0:1]
    for gi in range(tm // tg):
        rows = slice(gi * tg, (gi + 1) * tg)
        x = x_ref[0, rows, :]
        ms = jnp.mean(x * x, axis=-1, keepdims=True)
        h = (x * lax.rsqrt(ms + EPS) * g_ref[...]).astype(_BF16)
        for c, do_norm in enumerate(norm_chunks):
            acc = _dot(h, w_ref[:, c * PROJ_CHUNK:(c + 1) * PROJ_CHUNK])
            for s in range(PROJ_CHUNK // LANES):
                c0 = c * PROJ_CHUNK + s * LANES
                y = acc[:, s * LANES:(s + 1) * LANES]
                if do_norm:
                    sq = y * y
                    s_lo = jnp.sum(jnp.where(lo_mask, sq, 0.0), axis=-1, keepdims=True)
                    s_hi = jnp.sum(jnp.where(lo_mask, 0.0, sq), axis=-1, keepdims=True)
                    msq = jnp.where(lo_mask, s_lo, s_hi) * (1.0 / HEAD_DIM)
                    y = y * lax.rsqrt(msq + EPS) * gain_ref[:, c0:c0 + LANES]
                qkv_ref[0, rows, c0:c0 + LANES] = y.astype(_BF16)

        zT = bf_ref[...] + lax.dot_general(wfT_ref[...], h, (((1,), (1,)), ((), ())),
                                           preferred_element_type=_F32)
        lfT = (jnp.minimum(zT, 0.0) - jnp.log1p(jnp.exp(-jnp.abs(zT)))) * LOG2E
        hi = lfT.astype(_BF16)
        r1 = lfT - hi.astype(_F32)
        mid = r1.astype(_BF16)
        lo = (r1 - mid.astype(_F32)).astype(_BF16)
        cumT = carry + ((_dot(hi, triu) + _dot(mid, triu)) + _dot(lo, triu))
        carry = cumT[:, tg - 1:tg]
        cum_ref[0, :, rows] = cumT[:n_f, :]
    carry_ref[...] = jnp.broadcast_to(carry, carry_ref.shape)


def _proj_call(x, g_attn, w_bf16, wfT, b_f_col, gain_row, norm_chunks, n_fox, tm):
    B, S, D = x.shape
    W = gain_row.shape[1]
    rows = wfT.shape[0]
    tg = min(tm, PROJ_GROUP)
    triu = jnp.triu(jnp.ones((tg, tg), _F32)).astype(_BF16)
    const = lambda shape: pl.BlockSpec(shape, lambda b, s: (0,) * len(shape))
    return pl.pallas_call(
        functools.partial(_proj_kernel, norm_chunks=norm_chunks),
        grid=(B, S // tm),
        in_specs=[pl.BlockSpec((1, tm, D), lambda b, s: (b, s, 0)),
                  const((1, D)), const(w_bf16.shape), const((rows, D)), const((rows, 1)),
                  const((1, W)), const((tg, tg))],
        out_specs=[pl.BlockSpec((1, tm, W), lambda b, s: (b, s, 0)),
                   pl.BlockSpec((1, n_fox, tm), lambda b, s: (b, 0, s))],
        out_shape=[jax.ShapeDtypeStruct((B, S, W), _BF16),
                   jax.ShapeDtypeStruct((B, n_fox, S), _F32)],
        scratch_shapes=[pltpu.VMEM((rows, LANES), _F32)],
        compiler_params=pltpu.CompilerParams(
            dimension_semantics=("parallel", "arbitrary"), vmem_limit_bytes=VMEM_LIMIT),
        name="proj_qknorm_cumsum",
    )(x, g_attn, w_bf16, wfT, b_f_col, gain_row, triu)


def _stage_heads(q_ref, v_ref, qT_scr, vT_scr, t):
    S = q_ref.shape[1]
    row = lax.broadcasted_iota(jnp.int32, (LANES, 1), 0)
    ones = jnp.ones((ONES_ROWS, S), _BF16)
    qT = q_ref[0, :, t * LANES:(t + 1) * LANES].T
    vT = v_ref[0, :, t * LANES:(t + 1) * LANES].T
    for hl in range(HEADS_PER_TILE):
        h = t * HEADS_PER_TILE + hl
        mine = (row >= hl * HEAD_DIM) & (row < (hl + 1) * HEAD_DIM)
        qT_scr[h] = jnp.where(mine, qT, 0).astype(_BF16)
        vT_scr[h, :HEAD_DIM, :] = vT[hl * HEAD_DIM:(hl + 1) * HEAD_DIM, :]
        vT_scr[h, HEAD_DIM:, :] = ones


def _attend(nb, tq, k_ref, qT_scr, vT_scr, oT_scr, prep, stage):
    tiles = qT_scr.shape[0] // HEADS_PER_TILE
    th = tq // 2
    pairs = [(t, qi, j) for t in range(tiles) for qi in range(nb)
             for j in [qi] + list(range(qi))]
    heads_of = lambda t: range(t * HEADS_PER_TILE, (t + 1) * HEADS_PER_TILE)

    def qk(t, qi, j):
        if qi == 0:
            stage(t)
        lanes = slice(t * LANES, (t + 1) * LANES)
        return {h: [_dot(k_ref[0, j * tq + u * th:j * tq + (u + 1) * th, lanes],
                         qT_scr[h, :, qi * tq:(qi + 1) * tq]) for u in range(2)]
                for h in heads_of(t)}

    ahead = [qk(*pr) for pr in pairs[:QK_AHEAD]]
    acc, m = {}, {}
    for idx, (t, qi, j) in enumerate(pairs):
        cur = ahead.pop(0)
        if idx + QK_AHEAD < len(pairs):
            ahead.append(qk(*pairs[idx + QK_AHEAD]))
        for h in heads_of(t):
            run = None if j == qi else m[h]
            ps, shifts = [], []
            for u in range(2):
                sT, r = prep(h, qi, j, u, cur[h][u])
                mx = jnp.max(sT, axis=0, keepdims=True)
                if r is not None:
                    mx = mx + r
                run = mx if run is None else jnp.maximum(run, mx)
                ps.append(jnp.exp2((sT - (run if r is None else run - r)).astype(_BF16)))
                shifts.append(run)
            ps[0] = ps[0] * jnp.exp2(shifts[0] - shifts[1]).astype(_BF16)
            pv = _dot(vT_scr[h, :, j * tq:(j + 1) * tq], jnp.concatenate(ps, axis=0))
            acc[h] = pv if j == qi else jnp.exp2(m[h] - run) * acc[h] + pv
            m[h] = run
            if j == max(qi - 1, 0):
                o = acc[h][:HEAD_DIM, :] * (1.0 / acc[h][HEAD_DIM:HEAD_DIM + 1, :])
                oT_scr[h * HEAD_DIM:(h + 1) * HEAD_DIM, qi * tq:(qi + 1) * tq] = o


def _attn_scratch(S):
    nh = ATTN_TILES * HEADS_PER_TILE
    return [pltpu.VMEM((nh, LANES, S), _BF16),
            pltpu.VMEM((nh, HEAD_DIM + ONES_ROWS, S), _BF16),
            pltpu.VMEM((ATTN_TILES * LANES, S), _F32)]


def _moba_kernel(far_ref, q_ref, k_ref, v_ref, nbias_ref, o_ref, qT_scr, vT_scr, oT_scr):
    S = q_ref.shape[1]
    tq = MOBA_BLOCK
    nb = S // tq
    nh = qT_scr.shape[0]
    head0 = pl.program_id(1) * nh
    stage = lambda t: _stage_heads(q_ref, v_ref, qT_scr, vT_scr, t)

    km_hi, km_lo = [], []
    for t in range(nh // HEADS_PER_TILE):
        rows = [jnp.mean(k_ref[0, j * tq:(j + 1) * tq, t * LANES:(t + 1) * LANES].astype(_F32),
                         axis=0, keepdims=True) for j in range(nb)]
        rows += [jnp.zeros((1, LANES), _F32)] * (-nb % 16)
        km = jnp.concatenate(rows, axis=0)
        km_hi.append(km.astype(_BF16))
        km_lo.append((km - km_hi[-1].astype(_F32)).astype(_BF16))
    row = lax.broadcasted_iota(jnp.int32, (km_hi[0].shape[0], tq), 0)
    far = [far_ref[head0 + h] for h in range(nh)]
    keep = {}

    def prep(h, qi, j, u, sT):
        th = tq // 2
        if j == qi:
            if qi > MOBA_TOPK and u == 0:
                w = qT_scr[h, :, qi * tq:(qi + 1) * tq]
                t = h // HEADS_PER_TILE
                ge = jnp.where(row < qi, _dot(km_hi[t], w) + _dot(km_lo[t], w), -jnp.inf)
                for jj in range(qi):
                    gj = ge[jj:jj + 1, :]
                    ahead = (ge > gj) | ((ge == gj) & (row < jj))
                    rank = jnp.sum(jnp.where(ahead, 1.0, 0.0), axis=0, keepdims=True)
                    keep[h, qi, jj] = jnp.where(rank < MOBA_TOPK, 0.0, NEG)
            return sT - nbias_ref[h, 0, u * th:(u + 1) * th, :], None
        if j == qi - 1:
            return sT - nbias_ref[h, 1, u * th:(u + 1) * th, :], keep.get((h, qi, j))
        return sT, (keep[h, qi, j] + far[h] if (h, qi, j) in keep else far[h])

    _attend(nb, tq, k_ref, qT_scr, vT_scr, oT_scr, prep, stage)
    o_ref[0] = oT_scr[...].T.astype(o_ref.dtype)


def _moba_call(qkv, bias_tab, rel_far, n_tiles, q_off, k_off, v_off):
    B, S, _ = qkv.shape
    assert all(v % ATTN_TILES == 0 for v in (n_tiles, q_off, k_off, v_off))
    blk = (1, S, ATTN_TILES * LANES)
    nh = ATTN_TILES * HEADS_PER_TILE
    return pl.pallas_call(
        _moba_kernel,
        grid=(B, n_tiles // ATTN_TILES),
        in_specs=[pl.BlockSpec(memory_space=pltpu.SMEM),
                  pl.BlockSpec(blk, lambda b, t: (b, 0, q_off // ATTN_TILES + t)),
                  pl.BlockSpec(blk, lambda b, t: (b, 0, k_off // ATTN_TILES + t)),
                  pl.BlockSpec(blk, lambda b, t: (b, 0, v_off // ATTN_TILES + t)),
                  pl.BlockSpec((nh, 2, MOBA_BLOCK, MOBA_BLOCK), lambda b, t: (t, 0, 0, 0))],
        out_specs=pl.BlockSpec(blk, lambda b, t: (b, 0, t)),
        out_shape=jax.ShapeDtypeStruct((B, S, n_tiles * LANES), _BF16),
        scratch_shapes=_attn_scratch(S),
        compiler_params=pltpu.CompilerParams(
            dimension_semantics=("parallel", "arbitrary"), vmem_limit_bytes=VMEM_LIMIT),
        name="moba_attention",
    )(rel_far, qkv, qkv, qkv, bias_tab)


def _fox_kernel(q_ref, k_ref, v_ref, cum_ref, causal_ref, o_ref,
                qT_scr, vT_scr, oT_scr, ckb_scr):
    S = q_ref.shape[1]
    tq = causal_ref.shape[0]
    nb = S // tq
    head0 = pl.program_id(1) * qT_scr.shape[0]

    def stage(t):
        _stage_heads(q_ref, v_ref, qT_scr, vT_scr, t)
        for h in range(t * HEADS_PER_TILE, (t + 1) * HEADS_PER_TILE):
            ckb_scr[h] = jnp.broadcast_to(cum_ref[0, pl.ds(head0 + h, 1), :], (LANES, S)).T

    def prep(h, qi, j, u, sT):
        th = tq // 2
        ckb = ckb_scr[h, j * tq + u * th:j * tq + (u + 1) * th, :]
        sT = sT - jnp.concatenate([ckb, ckb], axis=1)
        if j == qi:
            sT = sT + causal_ref[u * th:(u + 1) * th, :]
        return sT, cum_ref[0, pl.ds(head0 + h, 1), qi * tq:(qi + 1) * tq]

    _attend(nb, tq, k_ref, qT_scr, vT_scr, oT_scr, prep, stage)
    o_ref[0] = oT_scr[...].T.astype(o_ref.dtype)


def _fox_call(qkv, cum, n_tiles, q_off, k_off, v_off, tq):
    B, S, _ = qkv.shape
    assert all(v % ATTN_TILES == 0 for v in (n_tiles, q_off, k_off, v_off))
    blk = (1, S, ATTN_TILES * LANES)
    t = jnp.arange(tq, dtype=jnp.int32)
    causal = jnp.where(t[:, None] <= t[None, :], 0.0, NEG).astype(_F32)
    return pl.pallas_call(
        _fox_kernel,
        grid=(B, n_tiles // ATTN_TILES),
        in_specs=[pl.BlockSpec(blk, lambda b, t: (b, 0, q_off // ATTN_TILES + t)),
                  pl.BlockSpec(blk, lambda b, t: (b, 0, k_off // ATTN_TILES + t)),
                  pl.BlockSpec(blk, lambda b, t: (b, 0, v_off // ATTN_TILES + t)),
                  pl.BlockSpec((1,) + cum.shape[1:], lambda b, t: (b, 0, 0)),
                  pl.BlockSpec((tq, tq), lambda b, t: (0, 0))],
        out_specs=pl.BlockSpec(blk, lambda b, t: (b, 0, t)),
        out_shape=jax.ShapeDtypeStruct((B, S, n_tiles * LANES), _BF16),
        scratch_shapes=_attn_scratch(S) + [pltpu.VMEM((ATTN_TILES * HEADS_PER_TILE, S, LANES), _F32)],
        compiler_params=pltpu.CompilerParams(
            dimension_semantics=("parallel", "arbitrary"), vmem_limit_bytes=VMEM_LIMIT),
        name="fox_attention",
    )(qkv, qkv, qkv, cum, causal)


def _rms(x, g):
    ms = jnp.mean(x * x, axis=-1, keepdims=True)
    return x * lax.rsqrt(ms + EPS) * g


def _mlp_kernel(x_ref, om_ref, of_ref, p_ref, wo_ref, gm_ref, wu_ref, wd_ref,
                gp_ref, wg_ref, wp_ref, out_ref, *, ff_chunk):
    wm = om_ref.shape[1]
    tm = x_ref.shape[0]
    groups = [slice(i * (tm // MLP_SPLIT), (i + 1) * (tm // MLP_SPLIT)) for i in range(MLP_SPLIT)]
    x1 = [x_ref[r, :] + (_dot(om_ref[r, :], wo_ref[:wm, :]) + _dot(of_ref[r, :], wo_ref[wm:, :]))
          for r in groups]
    h = [_rms(v, gm_ref[...]).astype(_BF16) for v in x1]
    y = [None] * MLP_SPLIT
    for c in range(wu_ref.shape[1] // ff_chunk):
        for i in range(MLP_SPLIT):
            u = _dot(h[i], wu_ref[:, c * ff_chunk:(c + 1) * ff_chunk])
            a = jnp.square(jnp.maximum(u, 0.0)).astype(_BF16)
            d = _dot(a, wd_ref[c * ff_chunk:(c + 1) * ff_chunk, :])
            y[i] = d if y[i] is None else y[i] + d
    for i, r in enumerate(groups):
        x2 = x1[i] + y[i]
        hg = _rms(x2, gp_ref[...]).astype(_BF16)
        gate = jax.nn.sigmoid(_dot(hg, wg_ref[...]))
        out_ref[r, :] = x2 + gate * _dot(p_ref[r, :].astype(_BF16), wp_ref[...])


def _mlp_call(x2d, o_m, o_f, p2d, w_out, g_mlp, w_up, w_down, g_ple, w_gate, w_proj, tm):
    T, D = x2d.shape
    row = lambda w: pl.BlockSpec((tm, w), lambda i: (i, 0))
    const = lambda a: pl.BlockSpec(a.shape, lambda i: (0, 0), pipeline_mode=pl.Buffered(1))
    return pl.pallas_call(
        functools.partial(_mlp_kernel, ff_chunk=512),
        grid=(T // tm,),
        in_specs=[row(D), row(o_m.shape[1]), row(o_f.shape[1]), row(p2d.shape[1]),
                  const(w_out), const(g_mlp), const(w_up), const(w_down),
                  const(g_ple), const(w_gate), const(w_proj)],
        out_specs=row(D),
        out_shape=jax.ShapeDtypeStruct((T, D), _F32),
        compiler_params=pltpu.CompilerParams(
            dimension_semantics=("parallel",), vmem_limit_bytes=VMEM_LIMIT),
        name="outproj_mlp_ple",
    )(x2d, o_m, o_f, p2d, w_out, g_mlp, w_up, w_down, g_ple, w_gate, w_proj)


def kernel(x, p, rel_bias, g_attn, w_in, b_f, gq_moba, gk_moba, gq_fox, gk_fox,
           w_out, g_mlp, w_up, w_down, g_ple, w_ple_gate, w_ple_proj):
    B, S, D = x.shape
    depth = p.shape[0]
    n_fox = b_f.shape[1]
    n_moba = rel_bias.shape[1]
    w_moba, w_fox = n_moba * HEAD_DIM, n_fox * HEAD_DIM
    w_main = 3 * w_moba + 3 * w_fox
    assert w_in.shape[2] == w_main + n_fox
    assert S % MOBA_BLOCK == 0 and w_moba % PROJ_CHUNK == 0 and w_fox % PROJ_CHUNK == 0
    assert n_fox <= LANES and n_fox % HEADS_PER_TILE == 0 and n_moba % HEADS_PER_TILE == 0

    assert MOBA_BLOCK + 1 >= MAX_DISTANCE
    rel_far = rel_bias[N_BUCKETS - 1] * LOG2E
    bias_tab = _bias_tables(rel_bias)

    cm, cf = w_moba // PROJ_CHUNK, w_fox // PROJ_CHUNK
    norm_chunks = (True,) * (2 * cm) + (False,) * cm + (True,) * (2 * cf) + (False,) * cf
    tiles_m, tiles_f = w_moba // LANES, w_fox // LANES
    ones_m, ones_f = jnp.ones((w_moba,), _F32), jnp.ones((w_fox,), _F32)

    for i in range(depth):
        gain_row = jnp.concatenate([
            jnp.tile(gq_moba[i] * (SCALE * LOG2E), n_moba), jnp.tile(gk_moba[i], n_moba), ones_m,
            jnp.tile(gq_fox[i] * (SCALE * LOG2E), n_fox), jnp.tile(gk_fox[i], n_fox), ones_f])[None, :]
        w_i = w_in[i].astype(_BF16)
        pad_f = -n_fox % 16
        wfT = jnp.pad(w_i[:, w_main:].T, ((0, pad_f), (0, 0)))
        b_col = jnp.pad(b_f[i], (0, pad_f))[:, None]
        qkv, cum = _proj_call(x, g_attn[i][None, :], w_i, wfT, b_col, gain_row,
                              norm_chunks, n_fox, tm=1024)

        o_m = _moba_call(qkv, bias_tab, rel_far, tiles_m, 0, tiles_m, 2 * tiles_m)
        f0 = 3 * tiles_m
        o_f = _fox_call(qkv, cum, tiles_f, f0, f0 + tiles_f, f0 + 2 * tiles_f, tq=256)

        x = _mlp_call(
            x.reshape(B * S, D), o_m.reshape(B * S, w_moba), o_f.reshape(B * S, w_fox),
            p[i].reshape(B * S, -1), w_out[i].astype(_BF16), g_mlp[i][None, :],
            w_up[i].astype(_BF16), w_down[i].astype(_BF16), g_ple[i][None, :],
            w_ple_gate[i].astype(_BF16), w_ple_proj[i].astype(_BF16), tm=1024,
        ).reshape(B, S, D)
    return x
```

```python
import functools

import numpy as np
import jax
import jax.numpy as jnp
from jax import lax
from jax.experimental import pallas as pl
from jax.experimental.pallas import tpu as pltpu

HEAD_DIM = 64
MOBA_BLOCK = 256
MOBA_TOPK = 3
N_BUCKETS = 32
MAX_DISTANCE = 128
EPS = 1e-6
NEG = -1e30
SCALE = HEAD_DIM ** -0.5
LOG2E = 1.4426950408889634

LANES = 128
HEADS_PER_TILE = LANES // HEAD_DIM
PROJ_CHUNK = 512
PROJ_GROUP = 512
QK_AHEAD = 2
ONES_ROWS = 16
ATTN_TILES = 2
MLP_SPLIT = 2
VMEM_LIMIT = 56 * 1024 * 1024

_F32 = jnp.float32
_BF16 = jnp.bfloat16


def _dot(a, b):
    return jnp.dot(a, b, preferred_element_type=_F32)


def _t5_bucket(rel):
    rel = np.maximum(np.asarray(rel), 0)
    max_exact = N_BUCKETS // 2
    relf = np.maximum(rel, max_exact).astype(np.float32)
    large = max_exact + (np.log(relf / max_exact) / np.log(MAX_DISTANCE / max_exact)
                         * (N_BUCKETS - max_exact)).astype(np.int32)
    large = np.minimum(large, N_BUCKETS - 1)
    return np.where(rel < max_exact, rel, large).astype(np.int32)


def _bias_kernel(rbT_ref, bucket_ref, out_ref):
    n_heads, n_dist = rbT_ref.shape[0], bucket_ref.shape[1]
    blk = n_dist // 2
    bucket = bucket_ref[...]
    vals = jnp.zeros((n_heads, n_dist), _F32)
    for b in range(N_BUCKETS):
        vals = jnp.where(bucket == b, rbT_ref[:, b:b + 1], vals)
    vals = vals * -LOG2E
    key = lax.broadcasted_iota(jnp.int32, (blk, blk), 0)
    query = lax.broadcasted_iota(jnp.int32, (blk, blk), 1)
    for h in range(n_heads):
        skew = pltpu.roll(jnp.broadcast_to(vals[h:h + 1, :], (blk, n_dist)), 0, 1,
                          stride=1, stride_axis=0)
        out_ref[h, 0] = jnp.where(key <= query, skew[:, :blk], -NEG)
        out_ref[h, 1] = skew[:, blk:]


def _bias_tables(rel_bias):
    n_heads = rel_bias.shape[1]
    bucket = jnp.asarray(_t5_bucket(np.arange(2 * MOBA_BLOCK))[None, :])
    blk = (MOBA_BLOCK, MOBA_BLOCK)
    return pl.pallas_call(
        _bias_kernel,
        out_shape=jax.ShapeDtypeStruct((n_heads, 2) + blk, _F32),
        name="moba_bias_tables",
    )(rel_bias.T, bucket)


def _proj_kernel(x_ref, g_ref, w_ref, wfT_ref, bf_ref, gain_ref, triu_ref,
                 qkv_ref, cum_ref, carry_ref, *, norm_chunks):
    tm = x_ref.shape[1]
    n_f = cum_ref.shape[1]

    @pl.when(pl.program_id(1) == 0)
    def _():
        carry_ref[...] = jnp.zeros_like(carry_ref)

    lane = lax.broadcasted_iota(jnp.int32, (1, LANES), 1)
    lo_mask = lane < HEAD_DIM
    triu = triu_ref[...]
    tg = triu.shape[0]
    carry = carry_ref[:, 0:1]
    for gi in range(tm // tg):
        rows = slice(gi * tg, (gi + 1) * tg)
        x = x_ref[0, rows, :]
        ms = jnp.mean(x * x, axis=-1, keepdims=True)
        h = (x * lax.rsqrt(ms + EPS) * g_ref[...]).astype(_BF16)
        for c, do_norm in enumerate(norm_chunks):
            acc = _dot(h, w_ref[:, c * PROJ_CHUNK:(c + 1) * PROJ_CHUNK])
            for s in range(PROJ_CHUNK // LANES):
                c0 = c * PROJ_CHUNK + s * LANES
                y = acc[:, s * LANES:(s + 1) * LANES]
                if do_norm:
                    sq = y * y
                    s_lo = jnp.sum(jnp.where(lo_mask, sq, 0.0), axis=-1, keepdims=True)
                    s_hi = jnp.sum(jnp.where(lo_mask, 0.0, sq), axis=-1, keepdims=True)
                    msq = jnp.where(lo_mask, s_lo, s_hi) * (1.0 / HEAD_DIM)
                    y = y * lax.rsqrt(msq + EPS) * gain_ref[:, c0:c0 + LANES]
                qkv_ref[0, rows, c0:c0 + LANES] = y.astype(_BF16)

        zT = bf_ref[...] + lax.dot_general(wfT_ref[...], h, (((1,), (1,)), ((), ())),
                                           preferred_element_type=_F32)
        lfT = (jnp.minimum(zT, 0.0) - jnp.log1p(jnp.exp(-jnp.abs(zT)))) * LOG2E
        hi = lfT.astype(_BF16)
        r1 = lfT - hi.astype(_F32)
        mid = r1.astype(_BF16)
        lo = (r1 - mid.astype(_F32)).astype(_BF16)
        cumT = carry + ((_dot(hi, triu) + _dot(mid, triu)) + _dot(lo, triu))
        carry = cumT[:, tg - 1:tg]
        cum_ref[0, :, rows] = cumT[:n_f, :]
    carry_ref[...] = jnp.broadcast_to(carry, carry_ref.shape)


def _proj_call(x, g_attn, w_bf16, wfT, b_f_col, gain_row, norm_chunks, n_fox, tm):
    B, S, D = x.shape
    W = gain_row.shape[1]
    rows = wfT.shape[0]
    tg = min(tm, PROJ_GROUP)
    triu = jnp.triu(jnp.ones((tg, tg), _F32)).astype(_BF16)
    const = lambda shape: pl.BlockSpec(shape, lambda b, s: (0,) * len(shape), pipeline_mode=pl.Buffered(1))
    return pl.pallas_call(
        functools.partial(_proj_kernel, norm_chunks=norm_chunks),
        grid=(B, S // tm),
        in_specs=[pl.BlockSpec((1, tm, D), lambda b, s: (b, s, 0)),
                  const((1, D)), const(w_bf16.shape), const((rows, D)), const((rows, 1)),
                  const((1, W)), const((tg, tg))],
        out_specs=[pl.BlockSpec((1, tm, W), lambda b, s: (b, s, 0)),
                   pl.BlockSpec((1, n_fox, tm), lambda b, s: (b, 0, s))],
        out_shape=[jax.ShapeDtypeStruct((B, S, W), _BF16),
                   jax.ShapeDtypeStruct((B, n_fox, S), _F32)],
        scratch_shapes=[pltpu.VMEM((rows, LANES), _F32)],
        compiler_params=pltpu.CompilerParams(
            dimension_semantics=("parallel", "arbitrary"), vmem_limit_bytes=VMEM_LIMIT),
        name="proj_qknorm_cumsum",
    )(x, g_attn, w_bf16, wfT, b_f_col, gain_row, triu)


def _stage_heads(q_ref, v_ref, qT_scr, vT_scr, t):
    S = q_ref.shape[1]
    row = lax.broadcasted_iota(jnp.int32, (LANES, 1), 0)
    ones = jnp.ones((ONES_ROWS, S), _BF16)
    qT = q_ref[0, :, t * LANES:(t + 1) * LANES].T
    vT = v_ref[0, :, t * LANES:(t + 1) * LANES].T
    for hl in range(HEADS_PER_TILE):
        h = t * HEADS_PER_TILE + hl
        mine = (row >= hl * HEAD_DIM) & (row < (hl + 1) * HEAD_DIM)
        qT_scr[h] = jnp.where(mine, qT, 0).astype(_BF16)
        vT_scr[h, :HEAD_DIM, :] = vT[hl * HEAD_DIM:(hl + 1) * HEAD_DIM, :]
        vT_scr[h, HEAD_DIM:, :] = ones


def _attend(nb, tq, k_ref, qT_scr, vT_scr, oT_scr, prep, stage):
    tiles = qT_scr.shape[0] // HEADS_PER_TILE
    th = tq // 2
    pairs = [(t, qi, j) for t in range(tiles) for qi in range(nb)
             for j in [qi] + list(range(qi))]
    heads_of = lambda t: range(t * HEADS_PER_TILE, (t + 1) * HEADS_PER_TILE)

    def qk(t, qi, j):
        if qi == 0:
            stage(t)
        lanes = slice(t * LANES, (t + 1) * LANES)
        return {h: [_dot(k_ref[0, j * tq + u * th:j * tq + (u + 1) * th, lanes],
                         qT_scr[h, :, qi * tq:(qi + 1) * tq]) for u in range(2)]
                for h in heads_of(t)}

    ahead = [qk(*pr) for pr in pairs[:QK_AHEAD]]
    acc, m = {}, {}
    for idx, (t, qi, j) in enumerate(pairs):
        cur = ahead.pop(0)
        if idx + QK_AHEAD < len(pairs):
            ahead.append(qk(*pairs[idx + QK_AHEAD]))
        for h in heads_of(t):
            run = None if j == qi else m[h]
            ps, shifts = [], []
            for u in range(2):
                sT, r = prep(h, qi, j, u, cur[h][u])
                mx = jnp.max(sT, axis=0, keepdims=True)
                if r is not None:
                    mx = mx + r
                run = mx if run is None else jnp.maximum(run, mx)
                ps.append(jnp.exp2((sT - (run if r is None else run - r)).astype(_BF16)))
                shifts.append(run)
            ps[0] = ps[0] * jnp.exp2(shifts[0] - shifts[1]).astype(_BF16)
            pv = _dot(vT_scr[h, :, j * tq:(j + 1) * tq], jnp.concatenate(ps, axis=0))
            acc[h] = pv if j == qi else jnp.exp2(m[h] - run) * acc[h] + pv
            m[h] = run
            if j == max(qi - 1, 0):
                o = acc[h][:HEAD_DIM, :] * (1.0 / acc[h][HEAD_DIM:HEAD_DIM + 1, :])
                oT_scr[h * HEAD_DIM:(h + 1) * HEAD_DIM, qi * tq:(qi + 1) * tq] = o


def _attn_scratch(S):
    nh = ATTN_TILES * HEADS_PER_TILE
    return [pltpu.VMEM((nh, LANES, S), _BF16),
            pltpu.VMEM((nh, HEAD_DIM + ONES_ROWS, S), _BF16),
            pltpu.VMEM((ATTN_TILES * LANES, S), _F32)]


def _moba_kernel(far_ref, q_ref, k_ref, v_ref, nbias_ref, o_ref, qT_scr, vT_scr, oT_scr):
    S = q_ref.shape[1]
    tq = MOBA_BLOCK
    nb = S // tq
    nh = qT_scr.shape[0]
    head0 = pl.program_id(1) * nh
    stage = lambda t: _stage_heads(q_ref, v_ref, qT_scr, vT_scr, t)

    km_hi, km_lo = [], []
    for t in range(nh // HEADS_PER_TILE):
        rows = [jnp.mean(k_ref[0, j * tq:(j + 1) * tq, t * LANES:(t + 1) * LANES].astype(_F32),
                         axis=0, keepdims=True) for j in range(nb)]
        rows += [jnp.zeros((1, LANES), _F32)] * (-nb % 16)
        km = jnp.concatenate(rows, axis=0)
        km_hi.append(km.astype(_BF16))
        km_lo.append((km - km_hi[-1].astype(_F32)).astype(_BF16))
    row = lax.broadcasted_iota(jnp.int32, (km_hi[0].shape[0], tq), 0)
    far = [far_ref[head0 + h] for h in range(nh)]
    keep = {}

    def prep(h, qi, j, u, sT):
        th = tq // 2
        if j == qi:
            if qi > MOBA_TOPK and u == 0:
                w = qT_scr[h, :, qi * tq:(qi + 1) * tq]
                t = h // HEADS_PER_TILE
                ge = jnp.where(row < qi, _dot(km_hi[t], w) + _dot(km_lo[t], w), -jnp.inf)
                for jj in range(qi):
                    gj = ge[jj:jj + 1, :]
                    ahead = (ge > gj) | ((ge == gj) & (row < jj))
                    rank = jnp.sum(jnp.where(ahead, 1.0, 0.0), axis=0, keepdims=True)
                    keep[h, qi, jj] = jnp.where(rank < MOBA_TOPK, 0.0, NEG)
            return sT - nbias_ref[h, 0, u * th:(u + 1) * th, :], None
        if j == qi - 1:
            return sT - nbias_ref[h, 1, u * th:(u + 1) * th, :], keep.get((h, qi, j))
        return sT, (keep[h, qi, j] + far[h] if (h, qi, j) in keep else far[h])

    _attend(nb, tq, k_ref, qT_scr, vT_scr, oT_scr, prep, stage)
    o_ref[0] = oT_scr[...].T.astype(o_ref.dtype)


def _moba_call(qkv, bias_tab, rel_far, n_tiles, q_off, k_off, v_off):
    B, S, _ = qkv.shape
    assert all(v % ATTN_TILES == 0 for v in (n_tiles, q_off, k_off, v_off))
    blk = (1, S, ATTN_TILES * LANES)
    nh = ATTN_TILES * HEADS_PER_TILE
    return pl.pallas_call(
        _moba_kernel,
        grid=(B, n_tiles // ATTN_TILES),
        in_specs=[pl.BlockSpec(memory_space=pltpu.SMEM),
                  pl.BlockSpec(blk, lambda b, t: (b, 0, q_off // ATTN_TILES + t)),
                  pl.BlockSpec(blk, lambda b, t: (b, 0, k_off // ATTN_TILES + t)),
                  pl.BlockSpec(blk, lambda b, t: (b, 0, v_off // ATTN_TILES + t)),
                  pl.BlockSpec((nh, 2, MOBA_BLOCK, MOBA_BLOCK), lambda b, t: (t, 0, 0, 0))],
        out_specs=pl.BlockSpec(blk, lambda b, t: (b, 0, t)),
        out_shape=jax.ShapeDtypeStruct((B, S, n_tiles * LANES), _BF16),
        scratch_shapes=_attn_scratch(S),
        compiler_params=pltpu.CompilerParams(
            dimension_semantics=("parallel", "arbitrary"), vmem_limit_bytes=VMEM_LIMIT),
        name="moba_attention",
    )(rel_far, qkv, qkv, qkv, bias_tab)


def _fox_kernel(q_ref, k_ref, v_ref, cum_ref, causal_ref, o_ref,
                qT_scr, vT_scr, oT_scr, ckb_scr):
    S = q_ref.shape[1]
    tq = causal_ref.shape[0]
    nb = S // tq
    head0 = pl.program_id(1) * qT_scr.shape[0]

    def stage(t):
        _stage_heads(q_ref, v_ref, qT_scr, vT_scr, t)
        for h in range(t * HEADS_PER_TILE, (t + 1) * HEADS_PER_TILE):
            ckb_scr[h] = jnp.broadcast_to(cum_ref[0, pl.ds(head0 + h, 1), :], (LANES, S)).T

    def prep(h, qi, j, u, sT):
        th = tq // 2
        ckb = ckb_scr[h, j * tq + u * th:j * tq + (u + 1) * th, :]
        sT = sT - jnp.concatenate([ckb, ckb], axis=1)
        if j == qi:
            sT = sT + causal_ref[u * th:(u + 1) * th, :]
        return sT, cum_ref[0, pl.ds(head0 + h, 1), qi * tq:(qi + 1) * tq]

    _attend(nb, tq, k_ref, qT_scr, vT_scr, oT_scr, prep, stage)
    o_ref[0] = oT_scr[...].T.astype(o_ref.dtype)


def _fox_call(qkv, cum, n_tiles, q_off, k_off, v_off, tq):
    B, S, _ = qkv.shape
    assert all(v % ATTN_TILES == 0 for v in (n_tiles, q_off, k_off, v_off))
    blk = (1, S, ATTN_TILES * LANES)
    t = jnp.arange(tq, dtype=jnp.int32)
    causal = jnp.where(t[:, None] <= t[None, :], 0.0, NEG).astype(_F32)
    return pl.pallas_call(
        _fox_kernel,
        grid=(B, n_tiles // ATTN_TILES),
        in_specs=[pl.BlockSpec(blk, lambda b, t: (b, 0, q_off // ATTN_TILES + t)),
                  pl.BlockSpec(blk, lambda b, t: (b, 0, k_off // ATTN_TILES + t)),
                  pl.BlockSpec(blk, lambda b, t: (b, 0, v_off // ATTN_TILES + t)),
                  pl.BlockSpec((1,) + cum.shape[1:], lambda b, t: (b, 0, 0)),
                  pl.BlockSpec((tq, tq), lambda b, t: (0, 0))],
        out_specs=pl.BlockSpec(blk, lambda b, t: (b, 0, t)),
        out_shape=jax.ShapeDtypeStruct((B, S, n_tiles * LANES), _BF16),
        scratch_shapes=_attn_scratch(S) + [pltpu.VMEM((ATTN_TILES * HEADS_PER_TILE, S, LANES), _F32)],
        compiler_params=pltpu.CompilerParams(
            dimension_semantics=("parallel", "arbitrary"), vmem_limit_bytes=VMEM_LIMIT),
        name="fox_attention",
    )(qkv, qkv, qkv, cum, causal)


def _rms(x, g):
    ms = jnp.mean(x * x, axis=-1, keepdims=True)
    return x * lax.rsqrt(ms + EPS) * g


def _mlp_kernel(x_ref, om_ref, of_ref, p_ref, wo_ref, gm_ref, wu_ref, wd_ref,
                gp_ref, wg_ref, wp_ref, out_ref, *, ff_chunk):
    wm = om_ref.shape[1]
    tm = x_ref.shape[0]
    groups = [slice(i * (tm // MLP_SPLIT), (i + 1) * (tm // MLP_SPLIT)) for i in range(MLP_SPLIT)]
    x1 = [x_ref[r, :] + (_dot(om_ref[r, :], wo_ref[:wm, :]) + _dot(of_ref[r, :], wo_ref[wm:, :]))
          for r in groups]
    h = [_rms(v, gm_ref[...]).astype(_BF16) for v in x1]
    y = [None] * MLP_SPLIT
    for c in range(wu_ref.shape[1] // ff_chunk):
        for i in range(MLP_SPLIT):
            u = _dot(h[i], wu_ref[:, c * ff_chunk:(c + 1) * ff_chunk])
            a = jnp.square(jnp.maximum(u, 0.0)).astype(_BF16)
            d = _dot(a, wd_ref[c * ff_chunk:(c + 1) * ff_chunk, :])
            y[i] = d if y[i] is None else y[i] + d
    for i, r in enumerate(groups):
        x2 = x1[i] + y[i]
        hg = _rms(x2, gp_ref[...]).astype(_BF16)
        gate = jax.nn.sigmoid(_dot(hg, wg_ref[...]))
        out_ref[r, :] = x2 + gate * _dot(p_ref[r, :].astype(_BF16), wp_ref[...])


def _mlp_call(x2d, o_m, o_f, p2d, w_out, g_mlp, w_up, w_down, g_ple, w_gate, w_proj, tm):
    T, D = x2d.shape
    row = lambda w: pl.BlockSpec((tm, w), lambda i: (i, 0))
    const = lambda a: pl.BlockSpec(a.shape, lambda i: (0, 0), pipeline_mode=pl.Buffered(1))
    return pl.pallas_call(
        functools.partial(_mlp_kernel, ff_chunk=512),
        grid=(T // tm,),
        in_specs=[row(D), row(o_m.shape[1]), row(o_f.shape[1]), row(p2d.shape[1]),
                  const(w_out), const(g_mlp), const(w_up), const(w_down),
                  const(g_ple), const(w_gate), const(w_proj)],
        out_specs=row(D),
        out_shape=jax.ShapeDtypeStruct((T, D), _F32),
        compiler_params=pltpu.CompilerParams(
            dimension_semantics=("parallel",), vmem_limit_bytes=VMEM_LIMIT),
        name="outproj_mlp_ple",
    )(x2d, o_m, o_f, p2d, w_out, g_mlp, w_up, w_down, g_ple, w_gate, w_proj)


def kernel(x, p, rel_bias, g_attn, w_in, b_f, gq_moba, gk_moba, gq_fox, gk_fox,
           w_out, g_mlp, w_up, w_down, g_ple, w_ple_gate, w_ple_proj):
    B, S, D = x.shape
    depth = p.shape[0]
    n_fox = b_f.shape[1]
    n_moba = rel_bias.shape[1]
    w_moba, w_fox = n_moba * HEAD_DIM, n_fox * HEAD_DIM
    w_main = 3 * w_moba + 3 * w_fox
    assert w_in.shape[2] == w_main + n_fox
    assert S % MOBA_BLOCK == 0 and w_moba % PROJ_CHUNK == 0 and w_fox % PROJ_CHUNK == 0
    assert n_fox <= LANES and n_fox % HEADS_PER_TILE == 0 and n_moba % HEADS_PER_TILE == 0

    assert MOBA_BLOCK + 1 >= MAX_DISTANCE
    rel_far = rel_bias[N_BUCKETS - 1] * LOG2E
    bias_tab = _bias_tables(rel_bias)

    cm, cf = w_moba // PROJ_CHUNK, w_fox // PROJ_CHUNK
    norm_chunks = (True,) * (2 * cm) + (False,) * cm + (True,) * (2 * cf) + (False,) * cf
    tiles_m, tiles_f = w_moba // LANES, w_fox // LANES
    ones_m, ones_f = jnp.ones((w_moba,), _F32), jnp.ones((w_fox,), _F32)

    for i in range(depth):
        gain_row = jnp.concatenate([
            jnp.tile(gq_moba[i] * (SCALE * LOG2E), n_moba), jnp.tile(gk_moba[i], n_moba), ones_m,
            jnp.tile(gq_fox[i] * (SCALE * LOG2E), n_fox), jnp.tile(gk_fox[i], n_fox), ones_f])[None, :]
        w_i = w_in[i].astype(_BF16)
        pad_f = -n_fox % 16
        wfT = jnp.pad(w_i[:, w_main:].T, ((0, pad_f), (0, 0)))
        b_col = jnp.pad(b_f[i], (0, pad_f))[:, None]
        qkv, cum = _proj_call(x, g_attn[i][None, :], w_i, wfT, b_col, gain_row,
                              norm_chunks, n_fox, tm=2048)

        o_m = _moba_call(qkv, bias_tab, rel_far, tiles_m, 0, tiles_m, 2 * tiles_m)
        f0 = 3 * tiles_m
        o_f = _fox_call(qkv, cum, tiles_f, f0, f0 + tiles_f, f0 + 2 * tiles_f, tq=256)

        x = _mlp_call(
            x.reshape(B * S, D), o_m.reshape(B * S, w_moba), o_f.reshape(B * S, w_fox),
            p[i].reshape(B * S, -1), w_out[i].astype(_BF16), g_mlp[i][None, :],
            w_up[i].astype(_BF16), w_down[i].astype(_BF16), g_ple[i][None, :],
            w_ple_gate[i].astype(_BF16), w_ple_proj[i].astype(_BF16), tm=1024,
        ).reshape(B, S, D)
    return x
```
